```python
import math
import jax, jax.numpy as jnp
from jax import lax
import numpy as np

D_MODEL = 2048
BATCH = 1
SEQ = 8192
DEPTH = 1

N_META = 16
MIX_WIDTH = D_MODEL
POOL_WIDTH = MIX_WIDTH // 2
CONV_WIDTH = MIX_WIDTH - POOL_WIDTH
POOL_WINDOWS = (2, 4, 8, 16)
N_POOL_GROUPS = len(POOL_WINDOWS)
POOL_GROUP_DIM = POOL_WIDTH // N_POOL_GROUPS
CONV_HEADS = 8
CONV_KSIZE = 31
IN_WIDTH = POOL_WIDTH + 2 * CONV_WIDTH
PEER_HEADS = 8
PEER_NKEYS = 128
PEER_EXPERTS = PEER_NKEYS * PEER_NKEYS
PEER_QDIM = 256
PEER_HALF = PEER_QDIM // 2
PEER_TOPK = 16
PEER_ROWS = 16
RMS_EPS = 1e-6
LN_EPS = 1e-5

kernel_name = "hymba_pool_conformer_peer_block"


def rms_norm(x, g):
    xf = x.astype(jnp.float32)
    y = xf * lax.rsqrt(jnp.mean(xf * xf, axis=-1, keepdims=True) + RMS_EPS)
    return (y * g.astype(jnp.float32)).astype(x.dtype)


def layer_norm(x, g, b):
    xf = x.astype(jnp.float32)
    mu = jnp.mean(xf, axis=-1, keepdims=True)
    xc = xf - mu
    var = jnp.mean(xc * xc, axis=-1, keepdims=True)
    y = xc * lax.rsqrt(var + LN_EPS) * g.astype(jnp.float32) + b.astype(jnp.float32)
    return y.astype(x.dtype)


def causal_multiscale_pool(xp):
    T = xp.shape[1]
    xf = xp.astype(jnp.float32)
    c0 = jnp.concatenate([jnp.zeros_like(xf[:, :1]), jnp.cumsum(xf, axis=1)], axis=1)
    t = jnp.arange(T, dtype=jnp.int32)
    win = jnp.array(POOL_WINDOWS, dtype=jnp.int32)
    lo = jnp.maximum(t[:, None] + 1 - win[None, :], 0)
    g_idx = jnp.arange(N_POOL_GROUPS, dtype=jnp.int32)[None, :]
    window_sum = c0[:, 1:] - c0[:, lo, g_idx]
    count = (t[:, None] + 1 - lo).astype(jnp.float32)
    mean = window_sum / count[None, :, :, None]
    return (mean - xf).astype(xp.dtype)


def causal_depthwise_conv(x, w, b):
    C = x.shape[-1]
    K = w.shape[0]
    y = lax.conv_general_dilated(
        x, w[:, None, :], window_strides=(1,), padding=[(K - 1, 0)],
        dimension_numbers=("NWC", "WIO", "NWC"), feature_group_count=C)
    return y + b


def peer_ffn(xn, wq, keys, down, up):
    B, T, D = xn.shape
    q = (xn @ wq).reshape(B, T, PEER_HEADS, 2, PEER_HALF)
    s = jnp.einsum("bthpd,hpnd->bthpn", q, keys).astype(jnp.float32)
    sv, si = lax.top_k(s, PEER_TOPK)
    cand_score = (sv[..., 0, :, None] + sv[..., 1, None, :]).reshape(B, T, PEER_HEADS, PEER_TOPK * PEER_TOPK)
    cand_idx = (si[..., 0, :, None] * PEER_NKEYS + si[..., 1, None, :]).reshape(B, T, PEER_HEADS, PEER_TOPK * PEER_TOPK)
    top_score, pos = lax.top_k(cand_score, PEER_TOPK)
    expert_idx = jnp.take_along_axis(cand_idx, pos, axis=-1)
    gate = jax.nn.softmax(top_score, axis=-1).astype(xn.dtype)

    R = B * T
    nb = R // PEER_ROWS

    def block(args):
        xb, ib, gb = args
        u = jnp.take(down, ib, axis=0)
        hidden = jnp.einsum("rd,rhkd->rhk", xb, u)
        act = gb * jax.nn.gelu(hidden, approximate=False)
        v = jnp.take(up, ib, axis=0)
        return jnp.einsum("rhk,rhkd->rd", act, v)

    out = lax.map(block, (xn.reshape(nb, PEER_ROWS, D),
                          expert_idx.reshape(nb, PEER_ROWS, PEER_HEADS, PEER_TOPK),
                          gate.reshape(nb, PEER_ROWS, PEER_HEADS, PEER_TOPK)))
    return out.reshape(B, T, D)


def setup_inputs(seed: int = 0) -> dict:
    key = jax.random.key(seed)
    ks = jax.random.split(key, 20)
    f32 = jnp.float32
    D, L = D_MODEL, DEPTH

    def nrm(k, shape, scale):
        return jax.random.normal(k, shape, f32) * scale

    return {
        "x": nrm(ks[0], (BATCH, SEQ, D), 1.0),
        "meta_tokens": nrm(ks[1], (N_META, D), 1.0),
        "mix_norm_g": 1.0 + nrm(ks[2], (L, D), 0.02),
        "w_in": nrm(ks[3], (L, D, IN_WIDTH), D ** -0.5),
        "pool_w": nrm(ks[4], (L, N_POOL_GROUPS, POOL_GROUP_DIM, POOL_GROUP_DIM), POOL_GROUP_DIM ** -0.5),
        "pool_scale": 1.0 + nrm(ks[5], (L, POOL_WIDTH), 0.02),
        "conv_dw_w": nrm(ks[6], (L, CONV_KSIZE, CONV_WIDTH), CONV_KSIZE ** -0.5),
        "conv_dw_b": nrm(ks[7], (L, CONV_WIDTH), 0.02),
        "conv_ln_g": 1.0 + nrm(ks[8], (L, CONV_WIDTH), 0.02),
        "conv_ln_b": nrm(ks[9], (L, CONV_WIDTH), 0.02),
        "conv_pw_w": nrm(ks[10], (L, CONV_WIDTH, CONV_WIDTH), CONV_WIDTH ** -0.5),
        "conv_pw_b": nrm(ks[11], (L, CONV_WIDTH), 0.02),
        "w_out": nrm(ks[12], (L, MIX_WIDTH, D), MIX_WIDTH ** -0.5),
        "ffn_norm_g": 1.0 + nrm(ks[13], (L, D), 0.02),
        "peer_wq": nrm(ks[14], (L, D, PEER_HEADS * PEER_QDIM), D ** -0.5),
        "peer_keys": nrm(ks[15], (L, PEER_HEADS, 2, PEER_NKEYS, PEER_HALF), PEER_HALF ** -0.5),
        "peer_down": nrm(ks[16], (L, PEER_EXPERTS, D), D ** -0.5),
        "peer_up": nrm(ks[17], (L, PEER_EXPERTS, D), 0.5),
        "final_norm_g": 1.0 + nrm(ks[18], (D,), 0.02),
    }


def reference(x, meta_tokens, mix_norm_g, w_in, pool_w, pool_scale, conv_dw_w, conv_dw_b,
              conv_ln_g, conv_ln_b, conv_pw_w, conv_pw_b, w_out, ffn_norm_g, peer_wq,
              peer_keys, peer_down, peer_up, final_norm_g):
    B = x.shape[0]
    meta = jnp.broadcast_to(meta_tokens[None].astype(x.dtype), (B, N_META, D_MODEL))
    h = jnp.concatenate([meta, x], axis=1)
    T = h.shape[1]
    for l in range(DEPTH):
        hn = rms_norm(h, mix_norm_g[l])
        proj = hn @ w_in[l]
        xp = proj[..., :POOL_WIDTH]
        xa = proj[..., POOL_WIDTH:POOL_WIDTH + CONV_WIDTH]
        xg = proj[..., POOL_WIDTH + CONV_WIDTH:]
        pooled = causal_multiscale_pool(xp.reshape(B, T, N_POOL_GROUPS, POOL_GROUP_DIM))
        ya = jnp.einsum("btgc,gcd->btgd", pooled, pool_w[l]).reshape(B, T, POOL_WIDTH) * pool_scale[l]
        glu = xa * jax.nn.sigmoid(xg)
        cv = causal_depthwise_conv(glu, conv_dw_w[l], conv_dw_b[l])
        cv = jax.nn.silu(layer_norm(cv, conv_ln_g[l], conv_ln_b[l]))
        yb = cv @ conv_pw_w[l] + conv_pw_b[l]
        h = h + jnp.concatenate([ya, yb], axis=-1) @ w_out[l]
        h = h + peer_ffn(rms_norm(h, ffn_norm_g[l]), peer_wq[l], peer_keys[l], peer_down[l], peer_up[l])
    h = rms_norm(h, final_norm_g)
    return h[:, N_META:]
```

```python
import functools
import math

import jax
import jax.numpy as jnp
from jax import lax
from jax.experimental import pallas as pl
from jax.experimental.pallas import tpu as pltpu

F32 = jnp.float32
BF16 = jnp.bfloat16

D_MODEL = 2048
SEQ = 8192
N_META = 16
POOL_WIDTH = 1024
CONV_WIDTH = 1024
IN_WIDTH = POOL_WIDTH + 2 * CONV_WIDTH
POOL_WINDOWS = (2, 4, 8, 16)
POOL_GROUP_DIM = 256
CONV_KSIZE = 31
PEER_HEADS = 8
PEER_NKEYS = 128
PEER_EXPERTS = PEER_NKEYS * PEER_NKEYS
PEER_HALF = 128
PEER_QW = PEER_HEADS * 2 * PEER_HALF
PEER_TOPK = 16
RMS_EPS = 1e-6
LN_EPS = 1e-5

LANES = 128
SUBLANES = 8
VMEM_LIMIT = 58 * 1024 * 1024

MIX_ROWS = 256
POOL_HALO = 16
CONV_HALO = 32
CONV_CHUNK = 32
PEER_ROWS = 512
PEER_TILES = PEER_ROWS // LANES
EXPERT_CHUNK = 1024
KEYS_PER_CHUNK = EXPERT_CHUNK // PEER_NKEYS
NEG_INF = float("-inf")


def _const_spec(shape):
    zeros = (0,) * len(shape)
    return pl.BlockSpec(shape, lambda *_: zeros, pipeline_mode=pl.Buffered(1))


def _mixer_kernel(x_ref, meta_ref, g1_ref, win_ref, poolw_ref, pscale_ref,
                  dww_ref, dwb_ref, lng_ref, lnb_ref, pww_ref, pwb_ref,
                  wout_ref, g2_ref,
                  h1_ref, xn2_ref,
                  xp_ext, glu_ext, cv_scr, ycat_scr):
    step = pl.program_id(0)
    rows = MIX_ROWS

    def project(tok):
        ms = jnp.mean(tok * tok, axis=-1, keepdims=True)
        hn = tok * lax.rsqrt(ms + RMS_EPS) * g1_ref[...]
        return jnp.dot(hn.astype(BF16), win_ref[...], preferred_element_type=F32)

    def glu_of(proj):
        xa = proj[:, POOL_WIDTH:POOL_WIDTH + CONV_WIDTH]
        xg = proj[:, POOL_WIDTH + CONV_WIDTH:]
        return xa * jax.nn.sigmoid(xg)

    @pl.when(step == 0)
    def _():
        pm = project(meta_ref[...])
        xp_ext[0:POOL_HALO, :] = pm[:, :POOL_WIDTH]
        glu_ext[0:CONV_HALO - N_META, :] = jnp.zeros((CONV_HALO - N_META, CONV_WIDTH), F32)
        glu_ext[CONV_HALO - N_META:CONV_HALO, :] = glu_of(pm)

    @pl.when(step > 0)
    def _():
        xp_ext[0:POOL_HALO, :] = xp_ext[rows:rows + POOL_HALO, :]
        glu_ext[0:CONV_HALO, :] = glu_ext[rows:rows + CONV_HALO, :]

    x = x_ref[...]
    proj = project(x)
    xp_ext[POOL_HALO:POOL_HALO + rows, :] = proj[:, :POOL_WIDTH]
    glu_ext[CONV_HALO:CONV_HALO + rows, :] = glu_of(proj)

    for g, win in enumerate(POOL_WINDOWS):
        c0 = g * POOL_GROUP_DIM
        cols = slice(c0, c0 + POOL_GROUP_DIM)
        cur = xp_ext[POOL_HALO:POOL_HALO + rows, cols]
        acc = cur
        for k in range(1, win):
            acc = acc + xp_ext[POOL_HALO - k:POOL_HALO - k + rows, cols]
        pooled = acc * (1.0 / win) - cur
        ya = jnp.dot(pooled.astype(BF16), poolw_ref[g], preferred_element_type=F32)
        ycat_scr[:, cols] = (ya * pscale_ref[:, cols]).astype(BF16)

    def conv_chunk(r, carry):
        r0 = pl.multiple_of(r * CONV_CHUNK, CONV_CHUNK)
        window = glu_ext[pl.ds(r0, CONV_CHUNK + CONV_HALO), :]
        acc = jnp.broadcast_to(dwb_ref[...], (CONV_CHUNK, CONV_WIDTH))
        for k in range(CONV_KSIZE):
            off = CONV_HALO - (CONV_KSIZE - 1) + k
            acc = acc + dww_ref[k:k + 1, :] * window[off:off + CONV_CHUNK, :]
        cv_scr[pl.ds(r0, CONV_CHUNK), :] = acc
        return carry

    lax.fori_loop(0, rows // CONV_CHUNK, conv_chunk, 0)

    cv = cv_scr[...]
    mu = jnp.mean(cv, axis=-1, keepdims=True)
    xc = cv - mu
    var = jnp.mean(xc * xc, axis=-1, keepdims=True)
    ln = xc * lax.rsqrt(var + LN_EPS) * lng_ref[...] + lnb_ref[...]
    sw = ln * jax.nn.sigmoid(ln)
    yb = jnp.dot(sw.astype(BF16), pww_ref[...], preferred_element_type=F32) + pwb_ref[...]
    ycat_scr[:, POOL_WIDTH:] = yb.astype(BF16)

    h1 = x + jnp.dot(ycat_scr[...], wout_ref[...], preferred_element_type=F32)
    h1_ref[...] = h1
    ms = jnp.mean(h1 * h1, axis=-1, keepdims=True)
    xn2_ref[...] = (h1 * lax.rsqrt(ms + RMS_EPS) * g2_ref[...]).astype(BF16)


def _mixer_call(x, meta, g1, w_in, pool_w, pool_scale, dw_w, dw_b, ln_g, ln_b,
                pw_w, pw_b, w_out, g2):
    n_steps = SEQ // MIX_ROWS
    row_spec = pl.BlockSpec((MIX_ROWS, D_MODEL), lambda i: (i, 0))
    return pl.pallas_call(
        _mixer_kernel,
        grid=(n_steps,),
        in_specs=[
            row_spec,
            _const_spec((N_META, D_MODEL)),
            _const_spec((1, D_MODEL)),
            _const_spec((D_MODEL, IN_WIDTH)),
            _const_spec((len(POOL_WINDOWS), POOL_GROUP_DIM, POOL_GROUP_DIM)),
            _const_spec((1, POOL_WIDTH)),
            _const_spec((CONV_KSIZE, CONV_WIDTH)),
            _const_spec((1, CONV_WIDTH)),
            _const_spec((1, CONV_WIDTH)),
            _const_spec((1, CONV_WIDTH)),
            _const_spec((CONV_WIDTH, CONV_WIDTH)),
            _const_spec((1, CONV_WIDTH)),
            _const_spec((D_MODEL, D_MODEL)),
            _const_spec((1, D_MODEL)),
        ],
        out_specs=[row_spec, row_spec],
        out_shape=[jax.ShapeDtypeStruct((SEQ, D_MODEL), F32),
                   jax.ShapeDtypeStruct((SEQ, D_MODEL), BF16)],
        scratch_shapes=[
            pltpu.VMEM((POOL_HALO + MIX_ROWS, POOL_WIDTH), F32),
            pltpu.VMEM((CONV_HALO + MIX_ROWS, CONV_WIDTH), F32),
            pltpu.VMEM((MIX_ROWS, CONV_WIDTH), F32),
            pltpu.VMEM((MIX_ROWS, D_MODEL), BF16),
        ],
        compiler_params=pltpu.CompilerParams(
            dimension_semantics=("arbitrary",), vmem_limit_bytes=VMEM_LIMIT),
        name="mixer",
    )(x, meta, g1, w_in, pool_w, pool_scale, dw_w, dw_b, ln_g, ln_b, pw_w, pw_b, w_out, g2)


def _top16(scores, sv_scr):
    cur = scores
    rank = jnp.full(scores.shape, float(PEER_TOPK), F32)
    for k in range(PEER_TOPK):
        m = jnp.max(cur, axis=0, keepdims=True)
        hit = cur == m
        rank = jnp.where(hit, float(k), rank)
        cur = jnp.where(hit, NEG_INF, cur)
        sv_scr[k:k + 1, :] = m
    return rank


def _route_tile(s1, s2, sv1_scr, sv2_scr):
    r1 = _top16(s1, sv1_scr)
    r2 = _top16(s2, sv2_scr)
    sv2_lo = sv2_scr[0:SUBLANES, :]
    row = lax.broadcasted_iota(jnp.int32, (SUBLANES, LANES), 0)

    cands = [sv1_scr[0:1, :] + sv2_scr[...]]
    for a in range(1, SUBLANES):
        c = sv1_scr[a:a + 1, :] + sv2_lo
        cands.append(jnp.where(row < PEER_TOPK // (a + 1), c, NEG_INF))
    cands.append(sv1_scr[SUBLANES:PEER_TOPK, :] + sv2_scr[0:1, :])

    top = cands[0][0:1, :]
    cur = list(cands)
    thr = top
    for k in range(PEER_TOPK):
        m = cur[0][0:SUBLANES, :]
        m = jnp.maximum(m, cur[0][SUBLANES:, :])
        for c in cur[1:]:
            m = jnp.maximum(m, c)
        thr = jnp.max(m, axis=0, keepdims=True)
        cur = [jnp.where(c == thr, NEG_INF, c) for c in cur]

    zsum = jnp.zeros((1, LANES), F32)
    counts = []
    for a, c in enumerate(cands):
        sel = c >= thr
        e = jnp.where(sel, jnp.exp(c - top), 0.0)
        zsum = zsum + jnp.sum(e, axis=0, keepdims=True)
        if a < SUBLANES:
            counts.append(jnp.sum(sel.astype(F32), axis=0, keepdims=True))
        else:
            tail = sel.astype(F32)
            counts.extend(tail[b:b + 1, :] for b in range(SUBLANES))

    n = jnp.zeros(s1.shape, F32)
    for a in range(PEER_TOPK):
        n = jnp.where(r1 == float(a), counts[a], n)

    inv_z = 1.0 / zsum
    u = jnp.where(r1 < float(PEER_TOPK), jnp.exp(s1 - sv1_scr[0:1, :]), 0.0)
    v = jnp.where(r2 < float(PEER_TOPK), jnp.exp(s2 - sv2_scr[0:1, :]) * inv_z, 0.0)
    return u, v, n, r2


def _route_kernel(xn_ref, wqt_ref, keys_ref, u_ref, v_ref, n_ref, r2_ref,
                  qt_scr, s_scr, sv1_scr, sv2_scr):
    qt = lax.dot_general(wqt_ref[...], xn_ref[...], (((1,), (1,)), ((), ())),
                         preferred_element_type=F32)
    qt_scr[...] = qt.astype(BF16)

    def head(h, carry):
        for p in range(2):
            q0 = pl.multiple_of((2 * h + p) * PEER_HALF, PEER_HALF)
            s_scr[p] = jnp.dot(keys_ref[h, p], qt_scr[pl.ds(q0, PEER_HALF), :],
                               preferred_element_type=F32)
        for t in range(PEER_TILES):
            lanes = slice(t * LANES, (t + 1) * LANES)
            u, v, n, r2 = _route_tile(s_scr[0, :, lanes], s_scr[1, :, lanes],
                                      sv1_scr, sv2_scr)
            u_ref[h, t] = u
            v_ref[h, t] = v
            n_ref[h, t] = n
            r2_ref[h, t] = r2
        return carry

    lax.fori_loop(0, PEER_HEADS, head, 0)


def _route_call(xn2, wq_t, keys):
    n_steps = SEQ // PEER_ROWS
    fac_shape = jax.ShapeDtypeStruct(
        (PEER_HEADS, SEQ // LANES, PEER_NKEYS, LANES), F32)
    fac_spec = pl.BlockSpec((PEER_HEADS, PEER_TILES, PEER_NKEYS, LANES),
                            lambda i: (0, i, 0, 0))
    return pl.pallas_call(
        _route_kernel,
        grid=(n_steps,),
        in_specs=[
            pl.BlockSpec((PEER_ROWS, D_MODEL), lambda i: (i, 0)),
            _const_spec((PEER_QW, D_MODEL)),
            _const_spec((PEER_HEADS, 2, PEER_NKEYS, PEER_HALF)),
        ],
        out_specs=[fac_spec] * 4,
        out_shape=[fac_shape] * 4,
        scratch_shapes=[
            pltpu.VMEM((PEER_QW, PEER_ROWS), BF16),
            pltpu.VMEM((2, PEER_NKEYS, PEER_ROWS), F32),
            pltpu.VMEM((PEER_TOPK, LANES), F32),
            pltpu.VMEM((PEER_TOPK, LANES), F32),
        ],
        compiler_params=pltpu.CompilerParams(
            dimension_semantics=("arbitrary",), vmem_limit_bytes=VMEM_LIMIT),
        name="peer_route",
    )(xn2, wq_t, keys)


def _expert_kernel(xn_ref, down_ref, upt_ref, u_ref, v_ref, n_ref, r2_ref,
                   h1_ref, gf_ref, out_ref, hid_scr, act_scr, acc_scr):
    c = pl.program_id(1)
    hid_scr[...] = lax.dot_general(down_ref[...], xn_ref[...],
                                   (((1,), (1,)), ((), ())),
                                   preferred_element_type=F32)

    inv_sqrt2 = 1.0 / math.sqrt(2.0)
    for il in range(KEYS_PER_CHUNK):
        erows = slice(il * PEER_NKEYS, (il + 1) * PEER_NKEYS)
        for t in range(PEER_TILES):
            lanes = slice(t * LANES, (t + 1) * LANES)
            w = jnp.zeros((PEER_NKEYS, LANES), F32)
            for h in range(PEER_HEADS):
                ui = u_ref[h, t, il:il + 1, :]
                ni = n_ref[h, t, il:il + 1, :]
                w = w + jnp.where(r2_ref[h, t] < ni, v_ref[h, t] * ui, 0.0)
            hid = hid_scr[erows, lanes]
            gelu = 0.5 * hid * (1.0 + lax.erf(hid * inv_sqrt2))
            act_scr[erows, lanes] = (w * gelu).astype(BF16)

    contrib = jnp.dot(upt_ref[...], act_scr[...], preferred_element_type=F32)

    @pl.when(c == 0)
    def _():
        acc_scr[...] = contrib

    @pl.when(c > 0)
    def _():
        acc_scr[...] += contrib

    @pl.when(c == pl.num_programs(1) - 1)
    def _():
        h2 = h1_ref[...] + acc_scr[...].T
        ms = jnp.mean(h2 * h2, axis=-1, keepdims=True)
        out_ref[...] = h2 * lax.rsqrt(ms + RMS_EPS) * gf_ref[...]


def _expert_call(xn2, down, up_t, u, v, n, r2, h1, gf):
    n_blocks = SEQ // PEER_ROWS
    n_chunks = PEER_EXPERTS // EXPERT_CHUNK
    row_spec = pl.BlockSpec((PEER_ROWS, D_MODEL), lambda b, c: (b, 0))
    key_spec = pl.BlockSpec((PEER_HEADS, PEER_TILES, KEYS_PER_CHUNK, LANES),
                            lambda b, c: (0, b, c, 0))
    full_spec = pl.BlockSpec((PEER_HEADS, PEER_TILES, PEER_NKEYS, LANES),
                             lambda b, c: (0, b, 0, 0))
    return pl.pallas_call(
        _expert_kernel,
        grid=(n_blocks, n_chunks),
        in_specs=[
            row_spec,
            pl.BlockSpec((EXPERT_CHUNK, D_MODEL), lambda b, c: (c, 0)),
            pl.BlockSpec((D_MODEL, EXPERT_CHUNK), lambda b, c: (0, c)),
            key_spec, full_spec, key_spec, full_spec,
            pl.BlockSpec((PEER_ROWS, D_MODEL), lambda b, c: (b, 0),
                         pipeline_mode=pl.Buffered(1)),
            pl.BlockSpec((1, D_MODEL), lambda b, c: (0, 0)),
        ],
        out_specs=row_spec,
        out_shape=jax.ShapeDtypeStruct((SEQ, D_MODEL), F32),
        scratch_shapes=[
            pltpu.VMEM((EXPERT_CHUNK, PEER_ROWS), F32),
            pltpu.VMEM((EXPERT_CHUNK, PEER_ROWS), BF16),
            pltpu.VMEM((D_MODEL, PEER_ROWS), F32),
        ],
        compiler_params=pltpu.CompilerParams(
            dimension_semantics=("arbitrary", "arbitrary"),
            vmem_limit_bytes=VMEM_LIMIT),
        name="peer_experts",
    )(xn2, down, up_t, u, v, n, r2, h1, gf)


def kernel(x, meta_tokens, mix_norm_g, w_in, pool_w, pool_scale, conv_dw_w, conv_dw_b, conv_ln_g, conv_ln_b, conv_pw_w, conv_pw_b, w_out, ffn_norm_g, peer_wq, peer_keys, peer_down, peer_up, final_norm_g):
    assert x.shape == (1, SEQ, D_MODEL) and w_in.shape[0] == 1
    row = lambda a: a.reshape(1, -1).astype(F32)
    h1, xn2 = _mixer_call(
        x[0], meta_tokens, row(mix_norm_g[0]), w_in[0].astype(BF16),
        pool_w[0].astype(BF16), row(pool_scale[0]), conv_dw_w[0], row(conv_dw_b[0]),
        row(conv_ln_g[0]), row(conv_ln_b[0]), conv_pw_w[0].astype(BF16),
        row(conv_pw_b[0]), w_out[0].astype(BF16), row(ffn_norm_g[0]))
    u, v, n, r2 = _route_call(xn2, peer_wq[0].T.astype(BF16), peer_keys[0].astype(BF16))
    out = _expert_call(xn2, peer_down[0].astype(BF16), peer_up[0].T.astype(BF16),
                       u, v, n, r2, h1, row(final_norm_g))
    return out[None]
```

```python
import functools
import math

import jax
import jax.numpy as jnp
from jax import lax
from jax.experimental import pallas as pl
from jax.experimental.pallas import tpu as pltpu

F32 = jnp.float32
BF16 = jnp.bfloat16

D_MODEL = 2048
SEQ = 8192
N_META = 16
POOL_WIDTH = 1024
CONV_WIDTH = 1024
IN_WIDTH = POOL_WIDTH + 2 * CONV_WIDTH
POOL_WINDOWS = (2, 4, 8, 16)
POOL_GROUP_DIM = 256
CONV_KSIZE = 31
PEER_HEADS = 8
PEER_NKEYS = 128
PEER_EXPERTS = PEER_NKEYS * PEER_NKEYS
PEER_HALF = 128
PEER_QW = PEER_HEADS * 2 * PEER_HALF
PEER_TOPK = 16
RMS_EPS = 1e-6
LN_EPS = 1e-5

LANES = 128
SUBLANES = 8
VMEM_LIMIT = 58 * 1024 * 1024

MIX_ROWS = 256
POOL_HALO = 16
CONV_HALO = 32
CONV_CHUNK = 32
PEER_ROWS = 512
PEER_TILES = PEER_ROWS // LANES
HALF_CHUNK = 512
HALF_KEYS = HALF_CHUNK // PEER_NKEYS
STEPS_PER_BLOCK = PEER_EXPERTS // (2 * HALF_CHUNK)
BF16_ROWS = 16
MXU_PIECE = 256
HID_K_PIECE = 512
UP_M_PIECE = 256
NEG_INF = float("-inf")


def _const_spec(shape):
    zeros = (0,) * len(shape)
    return pl.BlockSpec(shape, lambda *_: zeros, pipeline_mode=pl.Buffered(1))


def _mixer_kernel(x_ref, meta_ref, g1_ref, win_ref, poolw_ref, pscale_ref,
                  dww_ref, dwb_ref, lng_ref, lnb_ref, pww_ref, pwb_ref,
                  wout_ref, g2_ref,
                  h1_ref, xn2_ref,
                  xp_ext, glu_ext, cv_scr, ycat_scr):
    step = pl.program_id(0)
    rows = MIX_ROWS

    def project(tok):
        ms = jnp.mean(tok * tok, axis=-1, keepdims=True)
        hn = tok * lax.rsqrt(ms + RMS_EPS) * g1_ref[...]
        return jnp.dot(hn.astype(BF16), win_ref[...], preferred_element_type=F32)

    def glu_of(proj):
        xa = proj[:, POOL_WIDTH:POOL_WIDTH + CONV_WIDTH]
        xg = proj[:, POOL_WIDTH + CONV_WIDTH:]
        return xa * jax.nn.sigmoid(xg)

    @pl.when(step == 0)
    def _():
        pm = project(meta_ref[...])
        xp_ext[0:POOL_HALO, :] = pm[:, :POOL_WIDTH]
        glu_ext[0:CONV_HALO - N_META, :] = jnp.zeros((CONV_HALO - N_META, CONV_WIDTH), F32)
        glu_ext[CONV_HALO - N_META:CONV_HALO, :] = glu_of(pm)

    @pl.when(step > 0)
    def _():
        xp_ext[0:POOL_HALO, :] = xp_ext[rows:rows + POOL_HALO, :]
        glu_ext[0:CONV_HALO, :] = glu_ext[rows:rows + CONV_HALO, :]

    x = x_ref[...]
    proj = project(x)
    xp_ext[POOL_HALO:POOL_HALO + rows, :] = proj[:, :POOL_WIDTH]
    glu_ext[CONV_HALO:CONV_HALO + rows, :] = glu_of(proj)

    for g, win in enumerate(POOL_WINDOWS):
        c0 = g * POOL_GROUP_DIM
        cols = slice(c0, c0 + POOL_GROUP_DIM)
        cur = xp_ext[POOL_HALO:POOL_HALO + rows, cols]
        acc = cur
        for k in range(1, win):
            acc = acc + xp_ext[POOL_HALO - k:POOL_HALO - k + rows, cols]
        pooled = acc * (1.0 / win) - cur
        ya = jnp.dot(pooled.astype(BF16), poolw_ref[g], preferred_element_type=F32)
        ycat_scr[:, cols] = (ya * pscale_ref[:, cols]).astype(BF16)

    def conv_chunk(r, carry):
        r0 = pl.multiple_of(r * CONV_CHUNK, CONV_CHUNK)
        window = glu_ext[pl.ds(r0, CONV_CHUNK + CONV_HALO), :]
        acc = jnp.broadcast_to(dwb_ref[...], (CONV_CHUNK, CONV_WIDTH))
        for k in range(CONV_KSIZE):
            off = CONV_HALO - (CONV_KSIZE - 1) + k
            acc = acc + dww_ref[k:k + 1, :] * window[off:off + CONV_CHUNK, :]
        cv_scr[pl.ds(r0, CONV_CHUNK), :] = acc
        return carry

    lax.fori_loop(0, rows // CONV_CHUNK, conv_chunk, 0)

    cv = cv_scr[...]
    mu = jnp.mean(cv, axis=-1, keepdims=True)
    xc = cv - mu
    var = jnp.mean(xc * xc, axis=-1, keepdims=True)
    ln = xc * lax.rsqrt(var + LN_EPS) * lng_ref[...] + lnb_ref[...]
    sw = ln * jax.nn.sigmoid(ln)
    yb = jnp.dot(sw.astype(BF16), pww_ref[...], preferred_element_type=F32) + pwb_ref[...]
    ycat_scr[:, POOL_WIDTH:] = yb.astype(BF16)

    h1 = x + jnp.dot(ycat_scr[...], wout_ref[...], preferred_element_type=F32)
    h1_ref[...] = h1
    ms = jnp.mean(h1 * h1, axis=-1, keepdims=True)
    xn2_ref[...] = (h1 * lax.rsqrt(ms + RMS_EPS) * g2_ref[...]).astype(BF16)


def _mixer_call(x, meta, g1, w_in, pool_w, pool_scale, dw_w, dw_b, ln_g, ln_b,
                pw_w, pw_b, w_out, g2):
    n_steps = SEQ // MIX_ROWS
    row_spec = pl.BlockSpec((MIX_ROWS, D_MODEL), lambda i: (i, 0))
    return pl.pallas_call(
        _mixer_kernel,
        grid=(n_steps,),
        in_specs=[
            row_spec,
            _const_spec((N_META, D_MODEL)),
            _const_spec((1, D_MODEL)),
            _const_spec((D_MODEL, IN_WIDTH)),
            _const_spec((len(POOL_WINDOWS), POOL_GROUP_DIM, POOL_GROUP_DIM)),
            _const_spec((1, POOL_WIDTH)),
            _const_spec((CONV_KSIZE, CONV_WIDTH)),
            _const_spec((1, CONV_WIDTH)),
            _const_spec((1, CONV_WIDTH)),
            _const_spec((1, CONV_WIDTH)),
            _const_spec((CONV_WIDTH, CONV_WIDTH)),
            _const_spec((1, CONV_WIDTH)),
            _const_spec((D_MODEL, D_MODEL)),
            _const_spec((1, D_MODEL)),
        ],
        out_specs=[row_spec, row_spec],
        out_shape=[jax.ShapeDtypeStruct((SEQ, D_MODEL), F32),
                   jax.ShapeDtypeStruct((SEQ, D_MODEL), BF16)],
        scratch_shapes=[
            pltpu.VMEM((POOL_HALO + MIX_ROWS, POOL_WIDTH), F32),
            pltpu.VMEM((CONV_HALO + MIX_ROWS, CONV_WIDTH), F32),
            pltpu.VMEM((MIX_ROWS, CONV_WIDTH), F32),
            pltpu.VMEM((MIX_ROWS, D_MODEL), BF16),
        ],
        compiler_params=pltpu.CompilerParams(
            dimension_semantics=("arbitrary",), vmem_limit_bytes=VMEM_LIMIT),
        name="mixer",
    )(x, meta, g1, w_in, pool_w, pool_scale, dw_w, dw_b, ln_g, ln_b, pw_w, pw_b, w_out, g2)


def _top16(scores, sv_scr):
    cur = scores
    rank = jnp.full(scores.shape, float(PEER_TOPK), F32)
    for k in range(PEER_TOPK):
        m = jnp.max(cur, axis=0, keepdims=True)
        hit = cur == m
        rank = jnp.where(hit, float(k), rank)
        cur = jnp.where(hit, NEG_INF, cur)
        sv_scr[k:k + 1, :] = m
    return rank


def _route_tile(s1, s2, sv1_scr, sv2_scr):
    r1 = _top16(s1, sv1_scr)
    r2 = _top16(s2, sv2_scr)
    sv2_lo = sv2_scr[0:SUBLANES, :]
    row = lax.broadcasted_iota(jnp.int32, (SUBLANES, LANES), 0)

    cands = [sv1_scr[0:1, :] + sv2_scr[...]]
    for a in range(1, SUBLANES):
        c = sv1_scr[a:a + 1, :] + sv2_lo
        cands.append(jnp.where(row < PEER_TOPK // (a + 1), c, NEG_INF))
    cands.append(sv1_scr[SUBLANES:PEER_TOPK, :] + sv2_scr[0:1, :])

    top = cands[0][0:1, :]
    cur = list(cands)
    thr = top
    for k in range(PEER_TOPK):
        m = cur[0][0:SUBLANES, :]
        m = jnp.maximum(m, cur[0][SUBLANES:, :])
        for c in cur[1:]:
            m = jnp.maximum(m, c)
        thr = jnp.max(m, axis=0, keepdims=True)
        cur = [jnp.where(c == thr, NEG_INF, c) for c in cur]

    zsum = jnp.zeros((1, LANES), F32)
    counts = []
    for a, c in enumerate(cands):
        sel = c >= thr
        e = jnp.where(sel, jnp.exp(c - top), 0.0)
        zsum = zsum + jnp.sum(e, axis=0, keepdims=True)
        if a < SUBLANES:
            counts.append(jnp.sum(sel.astype(F32), axis=0, keepdims=True))
        else:
            tail = sel.astype(F32)
            counts.extend(tail[b:b + 1, :] for b in range(SUBLANES))

    n = jnp.zeros(s1.shape, F32)
    for a in range(PEER_TOPK):
        n = jnp.where(r1 == float(a), counts[a], n)

    inv_z = 1.0 / zsum
    u = jnp.where(r1 < float(PEER_TOPK), jnp.exp(s1 - sv1_scr[0:1, :]), 0.0)
    v = jnp.where(r2 < float(PEER_TOPK), jnp.exp(s2 - sv2_scr[0:1, :]) * inv_z, 0.0)
    return u, v, n, r2


def _route_kernel(xn_ref, wqt_ref, keys_ref, u_ref, v_ref, n_ref, r2_ref,
                  qt_scr, s_scr, sv1_scr, sv2_scr):
    qt = lax.dot_general(wqt_ref[...], xn_ref[...], (((1,), (1,)), ((), ())),
                         preferred_element_type=F32)
    qt_scr[...] = qt.astype(BF16)

    def head(h, carry):
        for p in range(2):
            q0 = pl.multiple_of((2 * h + p) * PEER_HALF, PEER_HALF)
            s_scr[p] = jnp.dot(keys_ref[h, p], qt_scr[pl.ds(q0, PEER_HALF), :],
                               preferred_element_type=F32)
        for t in range(PEER_TILES):
            lanes = slice(t * LANES, (t + 1) * LANES)
            u, v, n, r2 = _route_tile(s_scr[0, :, lanes], s_scr[1, :, lanes],
                                      sv1_scr, sv2_scr)
            u_ref[h, t] = u
            v_ref[h, t] = v.astype(BF16)
            n_ref[h, t] = n
            r2_ref[h, t] = r2.astype(BF16)
        return carry

    lax.fori_loop(0, PEER_HEADS, head, 0)


def _route_call(xn2, wq_t, keys):
    n_steps = SEQ // PEER_ROWS
    fac_dims = (PEER_HEADS, SEQ // LANES, PEER_NKEYS, LANES)
    fac_spec = pl.BlockSpec((PEER_HEADS, PEER_TILES, PEER_NKEYS, LANES),
                            lambda i: (0, i, 0, 0))
    return pl.pallas_call(
        _route_kernel,
        grid=(n_steps,),
        in_specs=[
            pl.BlockSpec((PEER_ROWS, D_MODEL), lambda i: (i, 0)),
            _const_spec((PEER_QW, D_MODEL)),
            _const_spec((PEER_HEADS, 2, PEER_NKEYS, PEER_HALF)),
        ],
        out_specs=[fac_spec] * 4,
        out_shape=[jax.ShapeDtypeStruct(fac_dims, dt) for dt in (F32, BF16, F32, BF16)],
        scratch_shapes=[
            pltpu.VMEM((PEER_QW, PEER_ROWS), BF16),
            pltpu.VMEM((2, PEER_NKEYS, PEER_ROWS), F32),
            pltpu.VMEM((PEER_TOPK, LANES), F32),
            pltpu.VMEM((PEER_TOPK, LANES), F32),
        ],
        compiler_params=pltpu.CompilerParams(
            dimension_semantics=("arbitrary",), vmem_limit_bytes=VMEM_LIMIT),
        name="peer_route",
    )(xn2, wq_t, keys)


def _hidden_piece(p, kk, hid_ref, down_ref, xn_ref):
    toks = slice(p * MXU_PIECE, (p + 1) * MXU_PIECE)
    feat = slice(kk * HID_K_PIECE, (kk + 1) * HID_K_PIECE)
    part = lax.dot_general(down_ref[:, feat], xn_ref[toks, feat],
                           (((1,), (1,)), ((), ())), preferred_element_type=F32)
    if kk == 0:
        hid_ref[:, toks] = part
    else:
        hid_ref[:, toks] += part


def _up_piece(p, m, acc_ref, upt_ref, act_ref):
    toks = slice(p * MXU_PIECE, (p + 1) * MXU_PIECE)
    rows = slice(m * UP_M_PIECE, (m + 1) * UP_M_PIECE)
    acc_ref[rows, toks] += jnp.dot(upt_ref[rows, :], act_ref[:, toks],
                                   preferred_element_type=F32)


def _gate_tile(krow, il, t, hid_ref, act_ref, u_ref, n_ref, v_ref, r2_ref):
    erows = slice(il * PEER_NKEYS, (il + 1) * PEER_NKEYS)
    lanes = slice(t * LANES, (t + 1) * LANES)
    w = jnp.zeros((PEER_NKEYS // BF16_ROWS, BF16_ROWS, LANES), BF16)
    for h in range(PEER_HEADS):
        ui = jnp.broadcast_to(u_ref[h, t, krow:krow + 1, :], (BF16_ROWS, LANES)).astype(BF16)
        ni = jnp.broadcast_to(n_ref[h, t, krow:krow + 1, :], (BF16_ROWS, LANES)).astype(BF16)
        w = w + jnp.where(r2_ref[h, t] < ni[None], v_ref[h, t] * ui[None],
                          jnp.zeros((), BF16))
    hid = hid_ref[erows, lanes]
    gelu = 0.5 * hid * (1.0 + lax.erf(hid * (1.0 / math.sqrt(2.0))))
    act_ref[erows, lanes] = w.reshape(PEER_NKEYS, LANES) * gelu.astype(BF16)


def _pipeline_part(half, hid_cur, act_cur, hid_next, down_ref, xn_ref,
                   act_prev, upt_ref, acc_ref, u_ref, n_ref, v_ref, r2_ref):
    tiles = [(il, t) for il in range(HALF_KEYS) for t in range(PEER_TILES)]
    n_tok = PEER_ROWS // MXU_PIECE
    ups = [functools.partial(_up_piece, p, m, acc_ref, upt_ref, act_prev)
           for p in range(n_tok) for m in range(D_MODEL // UP_M_PIECE)]
    hids = [functools.partial(_hidden_piece, p, kk, hid_next, down_ref, xn_ref)
            for p in range(n_tok) for kk in range(D_MODEL // HID_K_PIECE)]
    assert len(ups) == len(tiles) and len(tiles) % len(hids) == 0
    every = len(tiles) // len(hids)
    for k, (il, t) in enumerate(tiles):
        ups[k]()
        if k % every == 0:
            hids[k // every]()
        _gate_tile(half * HALF_KEYS + il, il, t, hid_cur, act_cur,
                   u_ref, n_ref, v_ref, r2_ref)


def _expert_kernel(xna_ref, xnb_ref, dn0_ref, dna_ref, dnb_ref, upa_ref, upb_ref,
                   u_ref, n_ref, v_ref, r2_ref, h1_ref, gf_ref, out_ref,
                   hid0, hid1, act0, act1, acc):
    s = pl.program_id(0)

    @pl.when(s == 0)
    def _():
        for p in range(PEER_ROWS // MXU_PIECE):
            for kk in range(D_MODEL // HID_K_PIECE):
                _hidden_piece(p, kk, hid0, dn0_ref, xna_ref)
        act1[...] = jnp.zeros(act1.shape, BF16)
        acc[...] = jnp.zeros(acc.shape, F32)

    _pipeline_part(0, hid0, act0, hid1, dna_ref, xna_ref, act1, upa_ref, acc,
                   u_ref, n_ref, v_ref, r2_ref)

    @pl.when(s % STEPS_PER_BLOCK == 0)
    def _():
        @pl.when(s > 0)
        def _():
            h2 = h1_ref[...] + acc[...].T
            ms = jnp.mean(h2 * h2, axis=-1, keepdims=True)
            out_ref[...] = h2 * lax.rsqrt(ms + RMS_EPS) * gf_ref[...]

        acc[...] = jnp.zeros(acc.shape, F32)

    _pipeline_part(1, hid1, act1, hid0, dnb_ref, xnb_ref, act0, upb_ref, acc,
                   u_ref, n_ref, v_ref, r2_ref)


def _expert_call(xn2, down, up_t, u, v, n, r2, h1, gf):
    n_blocks = SEQ // PEER_ROWS
    n_steps = n_blocks * STEPS_PER_BLOCK + 1
    last = n_blocks - 1
    halves = PEER_EXPERTS // HALF_CHUNK

    def tok(f):
        return lambda s: (jnp.minimum(f(s) // halves, last), 0)

    xn_spec = lambda f: pl.BlockSpec((PEER_ROWS, D_MODEL), tok(f))
    dn_spec = lambda f: pl.BlockSpec((HALF_CHUNK, D_MODEL), lambda s: (f(s) % halves, 0))
    up_spec = lambda f: pl.BlockSpec((D_MODEL, HALF_CHUNK), lambda s: (0, f(s) % halves))
    key_spec = pl.BlockSpec(
        (PEER_HEADS, PEER_TILES, 2 * HALF_KEYS, LANES),
        lambda s: (0, jnp.minimum(s // STEPS_PER_BLOCK, last), s % STEPS_PER_BLOCK, 0))
    full_spec = pl.BlockSpec(
        (PEER_HEADS, PEER_TILES, PEER_NKEYS // BF16_ROWS, BF16_ROWS, LANES),
        lambda s: (0, jnp.minimum(s // STEPS_PER_BLOCK, last), 0, 0, 0))
    done = lambda s: jnp.maximum(2 * s - 1, 0)
    v5 = v.reshape(PEER_HEADS, SEQ // LANES, PEER_NKEYS // BF16_ROWS, BF16_ROWS, LANES)
    r5 = r2.reshape(v5.shape)
    return pl.pallas_call(
        _expert_kernel,
        grid=(n_steps,),
        in_specs=[
            xn_spec(lambda s: 2 * s + 1),
            xn_spec(lambda s: 2 * s + 2),
            pl.BlockSpec((HALF_CHUNK, D_MODEL), lambda s: (0, 0), pipeline_mode=pl.Buffered(1)),
            dn_spec(lambda s: 2 * s + 1),
            dn_spec(lambda s: 2 * s + 2),
            up_spec(lambda s: 2 * s + halves - 1),
            up_spec(lambda s: 2 * s),
            key_spec, key_spec, full_spec, full_spec,
            pl.BlockSpec((PEER_ROWS, D_MODEL), tok(done), pipeline_mode=pl.Buffered(1)),
            pl.BlockSpec((1, D_MODEL), lambda s: (0, 0)),
        ],
        out_specs=pl.BlockSpec((PEER_ROWS, D_MODEL), tok(done)),
        out_shape=jax.ShapeDtypeStruct((SEQ, D_MODEL), F32),
        scratch_shapes=[
            pltpu.VMEM((HALF_CHUNK, PEER_ROWS), F32),
            pltpu.VMEM((HALF_CHUNK, PEER_ROWS), F32),
            pltpu.VMEM((HALF_CHUNK, PEER_ROWS), BF16),
            pltpu.VMEM((HALF_CHUNK, PEER_ROWS), BF16),
            pltpu.VMEM((D_MODEL, PEER_ROWS), F32),
        ],
        compiler_params=pltpu.CompilerParams(
            dimension_semantics=("arbitrary",), vmem_limit_bytes=VMEM_LIMIT),
        name="peer_experts",
    )(xn2, xn2, down, down, down, up_t, up_t, u, n, v5, r5, h1, gf)


def kernel(x, meta_tokens, mix_norm_g, w_in, pool_w, pool_scale, conv_dw_w, conv_dw_b, conv_ln_g, conv_ln_b, conv_pw_w, conv_pw_b, w_out, ffn_norm_g, peer_wq, peer_keys, peer_down, peer_up, final_norm_g):
    assert x.shape == (1, SEQ, D_MODEL) and w_in.shape[0] == 1
    row = lambda a: a.reshape(1, -1).astype(F32)
    h1, xn2 = _mixer_call(
        x[0], meta_tokens, row(mix_norm_g[0]), w_in[0].astype(BF16),
        pool_w[0].astype(BF16), row(pool_scale[0]), conv_dw_w[0], row(conv_dw_b[0]),
        row(conv_ln_g[0]), row(conv_ln_b[0]), conv_pw_w[0].astype(BF16),
        row(conv_pw_b[0]), w_out[0].astype(BF16), row(ffn_norm_g[0]))
    u, v, n, r2 = _route_call(xn2, peer_wq[0].T.astype(BF16), peer_keys[0].astype(BF16))
    out = _expert_call(xn2, peer_down[0].astype(BF16), peer_up[0].T.astype(BF16),
                       u, v, n, r2, h1, row(final_norm_g))
    return out[None]
```

```python
import functools
import math

import jax
import jax.numpy as jnp
from jax import lax
from jax.experimental import pallas as pl
from jax.experimental.pallas import tpu as pltpu

F32 = jnp.float32
BF16 = jnp.bfloat16

D_MODEL = 2048
SEQ = 8192
N_META = 16
POOL_WIDTH = 1024
CONV_WIDTH = 1024
IN_WIDTH = POOL_WIDTH + 2 * CONV_WIDTH
POOL_WINDOWS = (2, 4, 8, 16)
POOL_GROUP_DIM = 256
CONV_KSIZE = 31
PEER_HEADS = 8
PEER_NKEYS = 128
PEER_EXPERTS = PEER_NKEYS * PEER_NKEYS
PEER_HALF = 128
PEER_QW = PEER_HEADS * 2 * PEER_HALF
PEER_TOPK = 16
RMS_EPS = 1e-6
LN_EPS = 1e-5

LANES = 128
SUBLANES = 8
VMEM_LIMIT = 58 * 1024 * 1024

MIX_ROWS = 256
POOL_HALO = 16
CONV_HALO = 32
CONV_CHUNK = 128
PEER_ROWS = 512
PEER_TILES = PEER_ROWS // LANES
HALF_CHUNK = 512
HALF_KEYS = HALF_CHUNK // PEER_NKEYS
STEPS_PER_BLOCK = PEER_EXPERTS // (2 * HALF_CHUNK)
BF16_ROWS = 16
MXU_PIECE = 256
HID_K_PIECE = 512
UP_M_PIECE = 256
NEG_INF = float("-inf")


def _const_spec(shape):
    zeros = (0,) * len(shape)
    return pl.BlockSpec(shape, lambda *_: zeros, pipeline_mode=pl.Buffered(1))


def _mixer_kernel(x_ref, meta_ref, g1_ref, win_ref, poolw_ref, pscale_ref,
                  dww_ref, dwb_ref, lng_ref, lnb_ref, pww_ref, pwb_ref,
                  wout_ref, g2_ref,
                  h1_ref, xn2_ref,
                  xp_ext, glu_ext, cv_scr, ycat_scr, shift_scr):
    step = pl.program_id(0)
    rows = MIX_ROWS

    def project(tok):
        ms = jnp.mean(tok * tok, axis=-1, keepdims=True)
        hn = tok * lax.rsqrt(ms + RMS_EPS) * g1_ref[...]
        return jnp.dot(hn.astype(BF16), win_ref[...], preferred_element_type=F32)

    def glu_of(proj):
        xa = proj[:, POOL_WIDTH:POOL_WIDTH + CONV_WIDTH]
        xg = proj[:, POOL_WIDTH + CONV_WIDTH:]
        return xa * jax.nn.sigmoid(xg)

    @pl.when(step == 0)
    def _():
        pm = project(meta_ref[...])
        xp_ext[0:POOL_HALO, :] = pm[:, :POOL_WIDTH]
        glu_ext[0:CONV_HALO - N_META, :] = jnp.zeros((CONV_HALO - N_META, CONV_WIDTH), F32)
        glu_ext[CONV_HALO - N_META:CONV_HALO, :] = glu_of(pm)

    @pl.when(step > 0)
    def _():
        xp_ext[0:POOL_HALO, :] = xp_ext[rows:rows + POOL_HALO, :]
        glu_ext[0:CONV_HALO, :] = glu_ext[rows:rows + CONV_HALO, :]

    x = x_ref[...]
    proj = project(x)
    xp_ext[POOL_HALO:POOL_HALO + rows, :] = proj[:, :POOL_WIDTH]
    glu_ext[CONV_HALO:CONV_HALO + rows, :] = glu_of(proj)

    for g, win in enumerate(POOL_WINDOWS):
        c0 = g * POOL_GROUP_DIM
        cols = slice(c0, c0 + POOL_GROUP_DIM)
        cur = xp_ext[POOL_HALO:POOL_HALO + rows, cols]
        acc = cur
        for k in range(1, win):
            acc = acc + xp_ext[POOL_HALO - k:POOL_HALO - k + rows, cols]
        pooled = acc * (1.0 / win) - cur
        ya = jnp.dot(pooled.astype(BF16), poolw_ref[g], preferred_element_type=F32)
        ycat_scr[:, cols] = (ya * pscale_ref[:, cols]).astype(BF16)

    span = CONV_CHUNK + CONV_HALO - SUBLANES

    def conv_chunk(r, carry):
        r0 = pl.multiple_of(r * CONV_CHUNK, CONV_CHUNK)
        window = glu_ext[pl.ds(r0, CONV_CHUNK + CONV_HALO), :]
        for b in range(1, SUBLANES):
            shift_scr[b - 1] = window[b:b + span, :]
        for lt in range(CONV_WIDTH // LANES):
            lanes = slice(lt * LANES, (lt + 1) * LANES)
            acc = jnp.broadcast_to(dwb_ref[:, lanes], (CONV_CHUNK, LANES))
            for k in range(CONV_KSIZE):
                a, b = divmod(CONV_HALO - (CONV_KSIZE - 1) + k, SUBLANES)
                if b == 0:
                    src = glu_ext[pl.ds(pl.multiple_of(r0 + a * SUBLANES, SUBLANES), CONV_CHUNK), lanes]
                else:
                    src = shift_scr[b - 1, a * SUBLANES:a * SUBLANES + CONV_CHUNK, lanes]
                acc = acc + dww_ref[k:k + 1, lanes] * src
            cv_scr[pl.ds(r0, CONV_CHUNK), lanes] = acc
        return carry

    lax.fori_loop(0, rows // CONV_CHUNK, conv_chunk, 0)

    cv = cv_scr[...]
    mu = jnp.mean(cv, axis=-1, keepdims=True)
    xc = cv - mu
    var = jnp.mean(xc * xc, axis=-1, keepdims=True)
    ln = xc * lax.rsqrt(var + LN_EPS) * lng_ref[...] + lnb_ref[...]
    sw = ln * jax.nn.sigmoid(ln)
    yb = jnp.dot(sw.astype(BF16), pww_ref[...], preferred_element_type=F32) + pwb_ref[...]
    ycat_scr[:, POOL_WIDTH:] = yb.astype(BF16)

    h1 = x + jnp.dot(ycat_scr[...], wout_ref[...], preferred_element_type=F32)
    h1_ref[...] = h1
    ms = jnp.mean(h1 * h1, axis=-1, keepdims=True)
    xn2_ref[...] = (h1 * lax.rsqrt(ms + RMS_EPS) * g2_ref[...]).astype(BF16)


def _mixer_call(x, meta, g1, w_in, pool_w, pool_scale, dw_w, dw_b, ln_g, ln_b,
                pw_w, pw_b, w_out, g2):
    n_steps = SEQ // MIX_ROWS
    row_spec = pl.BlockSpec((MIX_ROWS, D_MODEL), lambda i: (i, 0))
    return pl.pallas_call(
        _mixer_kernel,
        grid=(n_steps,),
        in_specs=[
            row_spec,
            _const_spec((N_META, D_MODEL)),
            _const_spec((1, D_MODEL)),
            _const_spec((D_MODEL, IN_WIDTH)),
            _const_spec((len(POOL_WINDOWS), POOL_GROUP_DIM, POOL_GROUP_DIM)),
            _const_spec((1, POOL_WIDTH)),
            _const_spec((CONV_KSIZE, CONV_WIDTH)),
            _const_spec((1, CONV_WIDTH)),
            _const_spec((1, CONV_WIDTH)),
            _const_spec((1, CONV_WIDTH)),
            _const_spec((CONV_WIDTH, CONV_WIDTH)),
            _const_spec((1, CONV_WIDTH)),
            _const_spec((D_MODEL, D_MODEL)),
            _const_spec((1, D_MODEL)),
        ],
        out_specs=[row_spec, row_spec],
        out_shape=[jax.ShapeDtypeStruct((SEQ, D_MODEL), F32),
                   jax.ShapeDtypeStruct((SEQ, D_MODEL), BF16)],
        scratch_shapes=[
            pltpu.VMEM((POOL_HALO + MIX_ROWS, POOL_WIDTH), F32),
            pltpu.VMEM((CONV_HALO + MIX_ROWS, CONV_WIDTH), F32),
            pltpu.VMEM((MIX_ROWS, CONV_WIDTH), F32),
            pltpu.VMEM((MIX_ROWS, D_MODEL), BF16),
            pltpu.VMEM((SUBLANES - 1, CONV_CHUNK + CONV_HALO - SUBLANES, CONV_WIDTH), F32),
        ],
        compiler_params=pltpu.CompilerParams(
            dimension_semantics=("arbitrary",), vmem_limit_bytes=VMEM_LIMIT),
        name="mixer",
    )(x, meta, g1, w_in, pool_w, pool_scale, dw_w, dw_b, ln_g, ln_b, pw_w, pw_b, w_out, g2)


def _top16_pair(scores, sv_scr):
    cur = scores
    rank = jnp.full(scores.shape, float(PEER_TOPK), F32)
    for k in range(PEER_TOPK):
        m = jnp.max(cur, axis=1, keepdims=True)
        hit = cur == m
        rank = jnp.where(hit, float(k), rank)
        cur = jnp.where(hit, NEG_INF, cur)
        sv_scr[:, k:k + 1, :] = m
    return rank


def _route_tile(s_pair, sv_scr):
    ranks = _top16_pair(s_pair, sv_scr)
    r1, r2 = ranks[0], ranks[1]
    s1, s2 = s_pair[0], s_pair[1]
    sv1_scr, sv2_scr = sv_scr.at[0], sv_scr.at[1]
    sv2_lo = sv2_scr[0:SUBLANES, :]
    row = lax.broadcasted_iota(jnp.int32, (SUBLANES, LANES), 0)

    cands = [sv1_scr[0:1, :] + sv2_scr[...]]
    for a in range(1, SUBLANES):
        c = sv1_scr[a:a + 1, :] + sv2_lo
        cands.append(jnp.where(row < PEER_TOPK // (a + 1), c, NEG_INF))
    cands.append(sv1_scr[SUBLANES:PEER_TOPK, :] + sv2_scr[0:1, :])

    top = cands[0][0:1, :]
    cur = list(cands)
    thr = top
    for k in range(PEER_TOPK):
        m = cur[0][0:SUBLANES, :]
        m = jnp.maximum(m, cur[0][SUBLANES:, :])
        for c in cur[1:]:
            m = jnp.maximum(m, c)
        thr = jnp.max(m, axis=0, keepdims=True)
        cur = [jnp.where(c == thr, NEG_INF, c) for c in cur]

    zsum = jnp.zeros((1, LANES), F32)
    counts = []
    for a, c in enumerate(cands):
        sel = c >= thr
        e = jnp.where(sel, jnp.exp(c - top), 0.0)
        zsum = zsum + jnp.sum(e, axis=0, keepdims=True)
        if a < SUBLANES:
            counts.append(jnp.sum(sel.astype(F32), axis=0, keepdims=True))
        else:
            tail = sel.astype(F32)
            counts.extend(tail[b:b + 1, :] for b in range(SUBLANES))

    n = jnp.zeros(s1.shape, F32)
    for a in range(PEER_TOPK):
        n = jnp.where(r1 == float(a), counts[a], n)

    inv_z = 1.0 / zsum
    u = jnp.where(r1 < float(PEER_TOPK), jnp.exp(s1 - sv1_scr[0:1, :]), 0.0)
    v = jnp.where(r2 < float(PEER_TOPK), jnp.exp(s2 - sv2_scr[0:1, :]) * inv_z, 0.0)
    return u, v, n, r2


def _route_kernel(xn_ref, wqt_ref, keys_ref, u_ref, v_ref, n_ref, r2_ref,
                  qt_scr, s_scr, sv_scr):
    qt = lax.dot_general(wqt_ref[...], xn_ref[...], (((1,), (1,)), ((), ())),
                         preferred_element_type=F32)
    qt_scr[...] = qt.astype(BF16)

    def head(h, carry):
        for p in range(2):
            q0 = pl.multiple_of((2 * h + p) * PEER_HALF, PEER_HALF)
            s_scr[p] = jnp.dot(keys_ref[h, p], qt_scr[pl.ds(q0, PEER_HALF), :],
                               preferred_element_type=F32)
        for t in range(PEER_TILES):
            lanes = slice(t * LANES, (t + 1) * LANES)
            u, v, n, r2 = _route_tile(s_scr[:, :, lanes], sv_scr)
            u_ref[h, t] = u
            v_ref[h, t] = v.astype(BF16)
            n_ref[h, t] = n
            r2_ref[h, t] = r2.astype(BF16)
        return carry

    lax.fori_loop(0, PEER_HEADS, head, 0)


def _route_call(xn2, wq_t, keys):
    n_steps = SEQ // PEER_ROWS
    fac_dims = (PEER_HEADS, SEQ // LANES, PEER_NKEYS, LANES)
    fac_spec = pl.BlockSpec((PEER_HEADS, PEER_TILES, PEER_NKEYS, LANES),
                            lambda i: (0, i, 0, 0))
    return pl.pallas_call(
        _route_kernel,
        grid=(n_steps,),
        in_specs=[
            pl.BlockSpec((PEER_ROWS, D_MODEL), lambda i: (i, 0)),
            _const_spec((PEER_QW, D_MODEL)),
            _const_spec((PEER_HEADS, 2, PEER_NKEYS, PEER_HALF)),
        ],
        out_specs=[fac_spec] * 4,
        out_shape=[jax.ShapeDtypeStruct(fac_dims, dt) for dt in (F32, BF16, F32, BF16)],
        scratch_shapes=[
            pltpu.VMEM((PEER_QW, PEER_ROWS), BF16),
            pltpu.VMEM((2, PEER_NKEYS, PEER_ROWS), F32),
            pltpu.VMEM((2, PEER_TOPK, LANES), F32),
        ],
        compiler_params=pltpu.CompilerParams(
            dimension_semantics=("arbitrary",), vmem_limit_bytes=VMEM_LIMIT),
        name="peer_route",
    )(xn2, wq_t, keys)


def _hidden_piece(p, kk, hid_ref, down_ref, xn_ref):
    toks = slice(p * MXU_PIECE, (p + 1) * MXU_PIECE)
    feat = slice(kk * HID_K_PIECE, (kk + 1) * HID_K_PIECE)
    part = lax.dot_general(down_ref[:, feat], xn_ref[toks, feat],
                           (((1,), (1,)), ((), ())), preferred_element_type=F32)
    if kk == 0:
        hid_ref[:, toks] = part
    else:
        hid_ref[:, toks] += part


def _up_piece(p, m, acc_ref, upt_ref, act_ref):
    toks = slice(p * MXU_PIECE, (p + 1) * MXU_PIECE)
    rows = slice(m * UP_M_PIECE, (m + 1) * UP_M_PIECE)
    acc_ref[rows, toks] += jnp.dot(upt_ref[rows, :], act_ref[:, toks],
                                   preferred_element_type=F32)


def _gate_tile(krow, il, t, hid_ref, act_ref, u_ref, n_ref, v_ref, r2_ref):
    erows = slice(il * PEER_NKEYS, (il + 1) * PEER_NKEYS)
    lanes = slice(t * LANES, (t + 1) * LANES)
    w = jnp.zeros((PEER_NKEYS // BF16_ROWS, BF16_ROWS, LANES), BF16)
    for h in range(PEER_HEADS):
        ui = jnp.broadcast_to(u_ref[h, t, krow:krow + 1, :], (BF16_ROWS, LANES)).astype(BF16)
        ni = jnp.broadcast_to(n_ref[h, t, krow:krow + 1, :], (BF16_ROWS, LANES)).astype(BF16)
        w = w + jnp.where(r2_ref[h, t] < ni[None], v_ref[h, t] * ui[None],
                          jnp.zeros((), BF16))
    hid = hid_ref[erows, lanes]
    gelu = 0.5 * hid * (1.0 + lax.erf(hid * (1.0 / math.sqrt(2.0))))
    act_ref[erows, lanes] = w.reshape(PEER_NKEYS, LANES) * gelu.astype(BF16)


def _pipeline_part(half, hid_cur, act_cur, hid_next, down_ref, xn_ref,
                   act_prev, upt_ref, acc_ref, u_ref, n_ref, v_ref, r2_ref):
    tiles = [(il, t) for il in range(HALF_KEYS) for t in range(PEER_TILES)]
    n_tok = PEER_ROWS // MXU_PIECE
    ups = [functools.partial(_up_piece, p, m, acc_ref, upt_ref, act_prev)
           for p in range(n_tok) for m in range(D_MODEL // UP_M_PIECE)]
    hids = [functools.partial(_hidden_piece, p, kk, hid_next, down_ref, xn_ref)
            for p in range(n_tok) for kk in range(D_MODEL // HID_K_PIECE)]
    assert len(ups) == len(tiles) and len(tiles) % len(hids) == 0
    every = len(tiles) // len(hids)
    for k, (il, t) in enumerate(tiles):
        ups[k]()
        if k % every == 0:
            hids[k // every]()
        _gate_tile(half * HALF_KEYS + il, il, t, hid_cur, act_cur,
                   u_ref, n_ref, v_ref, r2_ref)


def _expert_kernel(xna_ref, xnb_ref, dn0_ref, dna_ref, dnb_ref, upa_ref, upb_ref,
                   u_ref, n_ref, v_ref, r2_ref, h1_ref, gf_ref, out_ref,
                   hid0, hid1, act0, act1, acc):
    s = pl.program_id(0)

    @pl.when(s == 0)
    def _():
        for p in range(PEER_ROWS // MXU_PIECE):
            for kk in range(D_MODEL // HID_K_PIECE):
                _hidden_piece(p, kk, hid0, dn0_ref, xna_ref)
        act1[...] = jnp.zeros(act1.shape, BF16)
        acc[...] = jnp.zeros(acc.shape, F32)

    _pipeline_part(0, hid0, act0, hid1, dna_ref, xna_ref, act1, upa_ref, acc,
                   u_ref, n_ref, v_ref, r2_ref)

    @pl.when(s % STEPS_PER_BLOCK == 0)
    def _():
        @pl.when(s > 0)
        def _():
            h2 = h1_ref[...] + acc[...].T
            ms = jnp.mean(h2 * h2, axis=-1, keepdims=True)
            out_ref[...] = h2 * lax.rsqrt(ms + RMS_EPS) * gf_ref[...]

        acc[...] = jnp.zeros(acc.shape, F32)

    _pipeline_part(1, hid1, act1, hid0, dnb_ref, xnb_ref, act0, upb_ref, acc,
                   u_ref, n_ref, v_ref, r2_ref)


def _expert_call(xn2, down, up_t, u, v, n, r2, h1, gf):
    n_blocks = SEQ // PEER_ROWS
    n_steps = n_blocks * STEPS_PER_BLOCK + 1
    last = n_blocks - 1
    halves = PEER_EXPERTS // HALF_CHUNK

    def tok(f):
        return lambda s: (jnp.minimum(f(s) // halves, last), 0)

    xn_spec = lambda f: pl.BlockSpec((PEER_ROWS, D_MODEL), tok(f))
    dn_spec = lambda f: pl.BlockSpec((HALF_CHUNK, D_MODEL), lambda s: (f(s) % halves, 0))
    up_spec = lambda f: pl.BlockSpec((None, D_MODEL, HALF_CHUNK),
                                     lambda s: (f(s) % halves, 0, 0))
    key_spec = pl.BlockSpec(
        (PEER_HEADS, PEER_TILES, 2 * HALF_KEYS, LANES),
        lambda s: (0, jnp.minimum(s // STEPS_PER_BLOCK, last), s % STEPS_PER_BLOCK, 0))
    full_spec = pl.BlockSpec(
        (PEER_HEADS, PEER_TILES, PEER_NKEYS // BF16_ROWS, BF16_ROWS, LANES),
        lambda s: (0, jnp.minimum(s // STEPS_PER_BLOCK, last), 0, 0, 0))
    done = lambda s: jnp.maximum(2 * s - 1, 0)
    v5 = v.reshape(PEER_HEADS, SEQ // LANES, PEER_NKEYS // BF16_ROWS, BF16_ROWS, LANES)
    r5 = r2.reshape(v5.shape)
    return pl.pallas_call(
        _expert_kernel,
        grid=(n_steps,),
        in_specs=[
            xn_spec(lambda s: 2 * s + 1),
            xn_spec(lambda s: 2 * s + 2),
            pl.BlockSpec((HALF_CHUNK, D_MODEL), lambda s: (0, 0), pipeline_mode=pl.Buffered(1)),
            dn_spec(lambda s: 2 * s + 1),
            dn_spec(lambda s: 2 * s + 2),
            up_spec(lambda s: 2 * s + halves - 1),
            up_spec(lambda s: 2 * s),
            key_spec, key_spec, full_spec, full_spec,
            pl.BlockSpec((PEER_ROWS, D_MODEL), tok(done), pipeline_mode=pl.Buffered(1)),
            pl.BlockSpec((1, D_MODEL), lambda s: (0, 0)),
        ],
        out_specs=pl.BlockSpec((PEER_ROWS, D_MODEL), tok(done)),
        out_shape=jax.ShapeDtypeStruct((SEQ, D_MODEL), F32),
        scratch_shapes=[
            pltpu.VMEM((HALF_CHUNK, PEER_ROWS), F32),
            pltpu.VMEM((HALF_CHUNK, PEER_ROWS), F32),
            pltpu.VMEM((HALF_CHUNK, PEER_ROWS), BF16),
            pltpu.VMEM((HALF_CHUNK, PEER_ROWS), BF16),
            pltpu.VMEM((D_MODEL, PEER_ROWS), F32),
        ],
        compiler_params=pltpu.CompilerParams(
            dimension_semantics=("arbitrary",), vmem_limit_bytes=VMEM_LIMIT),
        name="peer_experts",
    )(xn2, xn2, down, down, down, up_t, up_t, u, n, v5, r5, h1, gf)


def kernel(x, meta_tokens, mix_norm_g, w_in, pool_w, pool_scale, conv_dw_w, conv_dw_b, conv_ln_g, conv_ln_b, conv_pw_w, conv_pw_b, w_out, ffn_norm_g, peer_wq, peer_keys, peer_down, peer_up, final_norm_g):
    assert x.shape == (1, SEQ, D_MODEL) and w_in.shape[0] == 1
    row = lambda a: a.reshape(1, -1).astype(F32)
    h1, xn2 = _mixer_call(
        x[0], meta_tokens, row(mix_norm_g[0]), w_in[0].astype(BF16),
        pool_w[0].astype(BF16), row(pool_scale[0]), conv_dw_w[0], row(conv_dw_b[0]),
        row(conv_ln_g[0]), row(conv_ln_b[0]), conv_pw_w[0].astype(BF16),
        row(conv_pw_b[0]), w_out[0].astype(BF16), row(ffn_norm_g[0]))
    u, v, n, r2 = _route_call(xn2, peer_wq[0].T.astype(BF16), peer_keys[0].astype(BF16))
    up_t = peer_up[0].astype(BF16).reshape(-1, HALF_CHUNK, D_MODEL).transpose(0, 2, 1)
    out = _expert_call(xn2, peer_down[0].astype(BF16), up_t,
                       u, v, n, r2, h1, row(final_norm_g))
    return out[None]
```

```python
import functools
import math

import jax
import jax.numpy as jnp
from jax import lax
from jax.experimental import pallas as pl
from jax.experimental.pallas import tpu as pltpu

F32 = jnp.float32
BF16 = jnp.bfloat16

D_MODEL = 2048
SEQ = 8192
N_META = 16
POOL_WIDTH = 1024
CONV_WIDTH = 1024
IN_WIDTH = POOL_WIDTH + 2 * CONV_WIDTH
POOL_WINDOWS = (2, 4, 8, 16)
POOL_GROUP_DIM = 256
CONV_KSIZE = 31
PEER_HEADS = 8
PEER_NKEYS = 128
PEER_EXPERTS = PEER_NKEYS * PEER_NKEYS
PEER_HALF = 128
PEER_QW = PEER_HEADS * 2 * PEER_HALF
PEER_TOPK = 16
RMS_EPS = 1e-6
LN_EPS = 1e-5

LANES = 128
SUBLANES = 8
VMEM_LIMIT = 58 * 1024 * 1024

MIX_ROWS = 256
POOL_HALO = 16
CONV_HALO = 32
CONV_CHUNK = 128
PEER_ROWS = 512
PEER_TILES = PEER_ROWS // LANES
HALF_CHUNK = 512
HALF_KEYS = HALF_CHUNK // PEER_NKEYS
STEPS_PER_BLOCK = PEER_EXPERTS // (2 * HALF_CHUNK)
MXU_PIECE = 256
HID_K_PIECE = 512
UP_M_PIECE = 256
NEG_INF = float("-inf")


def _const_spec(shape):
    zeros = (0,) * len(shape)
    return pl.BlockSpec(shape, lambda *_: zeros, pipeline_mode=pl.Buffered(1))


def _mixer_kernel(x_ref, meta_ref, g1_ref, win_ref, poolw_ref, pscale_ref,
                  dww_ref, dwb_ref, lng_ref, lnb_ref, pww_ref, pwb_ref,
                  wout_ref, g2_ref,
                  h1_ref, xn2_ref,
                  xp_ext, glu_ext, cv_scr, ycat_scr, shift_scr):
    step = pl.program_id(0)
    rows = MIX_ROWS

    def project(tok):
        ms = jnp.mean(tok * tok, axis=-1, keepdims=True)
        hn = tok * lax.rsqrt(ms + RMS_EPS) * g1_ref[...]
        return jnp.dot(hn.astype(BF16), win_ref[...], preferred_element_type=F32)

    def glu_of(proj):
        xa = proj[:, POOL_WIDTH:POOL_WIDTH + CONV_WIDTH]
        xg = proj[:, POOL_WIDTH + CONV_WIDTH:]
        return xa * jax.nn.sigmoid(xg)

    @pl.when(step == 0)
    def _():
        pm = project(meta_ref[...])
        xp_ext[0:POOL_HALO, :] = pm[:, :POOL_WIDTH]
        glu_ext[0:CONV_HALO - N_META, :] = jnp.zeros((CONV_HALO - N_META, CONV_WIDTH), F32)
        glu_ext[CONV_HALO - N_META:CONV_HALO, :] = glu_of(pm)

    @pl.when(step > 0)
    def _():
        xp_ext[0:POOL_HALO, :] = xp_ext[rows:rows + POOL_HALO, :]
        glu_ext[0:CONV_HALO, :] = glu_ext[rows:rows + CONV_HALO, :]

    x = x_ref[...]
    proj = project(x)
    xp_ext[POOL_HALO:POOL_HALO + rows, :] = proj[:, :POOL_WIDTH]
    glu_ext[CONV_HALO:CONV_HALO + rows, :] = glu_of(proj)

    for g, win in enumerate(POOL_WINDOWS):
        c0 = g * POOL_GROUP_DIM
        cols = slice(c0, c0 + POOL_GROUP_DIM)
        cur = xp_ext[POOL_HALO:POOL_HALO + rows, cols]
        acc = cur
        for k in range(1, win):
            acc = acc + xp_ext[POOL_HALO - k:POOL_HALO - k + rows, cols]
        pooled = acc * (1.0 / win) - cur
        ya = jnp.dot(pooled.astype(BF16), poolw_ref[g], preferred_element_type=F32)
        ycat_scr[:, cols] = (ya * pscale_ref[:, cols]).astype(BF16)

    span = CONV_CHUNK + CONV_HALO - SUBLANES

    def conv_chunk(r, carry):
        r0 = pl.multiple_of(r * CONV_CHUNK, CONV_CHUNK)
        window = glu_ext[pl.ds(r0, CONV_CHUNK + CONV_HALO), :]
        for b in range(1, SUBLANES):
            shift_scr[b - 1] = window[b:b + span, :]
        for lt in range(CONV_WIDTH // LANES):
            lanes = slice(lt * LANES, (lt + 1) * LANES)
            acc = jnp.broadcast_to(dwb_ref[:, lanes], (CONV_CHUNK, LANES))
            for k in range(CONV_KSIZE):
                a, b = divmod(CONV_HALO - (CONV_KSIZE - 1) + k, SUBLANES)
                if b == 0:
                    src = glu_ext[pl.ds(pl.multiple_of(r0 + a * SUBLANES, SUBLANES), CONV_CHUNK), lanes]
                else:
                    src = shift_scr[b - 1, a * SUBLANES:a * SUBLANES + CONV_CHUNK, lanes]
                acc = acc + dww_ref[k:k + 1, lanes] * src
            cv_scr[pl.ds(r0, CONV_CHUNK), lanes] = acc
        return carry

    lax.fori_loop(0, rows // CONV_CHUNK, conv_chunk, 0)

    cv = cv_scr[...]
    mu = jnp.mean(cv, axis=-1, keepdims=True)
    xc = cv - mu
    var = jnp.mean(xc * xc, axis=-1, keepdims=True)
    ln = xc * lax.rsqrt(var + LN_EPS) * lng_ref[...] + lnb_ref[...]
    sw = ln * jax.nn.sigmoid(ln)
    yb = jnp.dot(sw.astype(BF16), pww_ref[...], preferred_element_type=F32) + pwb_ref[...]
    ycat_scr[:, POOL_WIDTH:] = yb.astype(BF16)

    h1 = x + jnp.dot(ycat_scr[...], wout_ref[...], preferred_element_type=F32)
    h1_ref[...] = h1
    ms = jnp.mean(h1 * h1, axis=-1, keepdims=True)
    xn2_ref[...] = (h1 * lax.rsqrt(ms + RMS_EPS) * g2_ref[...]).T.astype(BF16)


def _mixer_call(x, meta, g1, w_in, pool_w, pool_scale, dw_w, dw_b, ln_g, ln_b,
                pw_w, pw_b, w_out, g2):
    n_steps = SEQ // MIX_ROWS
    row_spec = pl.BlockSpec((MIX_ROWS, D_MODEL), lambda i: (i, 0))
    return pl.pallas_call(
        _mixer_kernel,
        grid=(n_steps,),
        in_specs=[
            row_spec,
            _const_spec((N_META, D_MODEL)),
            _const_spec((1, D_MODEL)),
            _const_spec((D_MODEL, IN_WIDTH)),
            _const_spec((len(POOL_WINDOWS), POOL_GROUP_DIM, POOL_GROUP_DIM)),
            _const_spec((1, POOL_WIDTH)),
            _const_spec((CONV_KSIZE, CONV_WIDTH)),
            _const_spec((1, CONV_WIDTH)),
            _const_spec((1, CONV_WIDTH)),
            _const_spec((1, CONV_WIDTH)),
            _const_spec((CONV_WIDTH, CONV_WIDTH)),
            _const_spec((1, CONV_WIDTH)),
            _const_spec((D_MODEL, D_MODEL)),
            _const_spec((1, D_MODEL)),
        ],
        out_specs=[row_spec, pl.BlockSpec((D_MODEL, MIX_ROWS), lambda i: (0, i))],
        out_shape=[jax.ShapeDtypeStruct((SEQ, D_MODEL), F32),
                   jax.ShapeDtypeStruct((D_MODEL, SEQ), BF16)],
        scratch_shapes=[
            pltpu.VMEM((POOL_HALO + MIX_ROWS, POOL_WIDTH), F32),
            pltpu.VMEM((CONV_HALO + MIX_ROWS, CONV_WIDTH), F32),
            pltpu.VMEM((MIX_ROWS, CONV_WIDTH), F32),
            pltpu.VMEM((MIX_ROWS, D_MODEL), BF16),
            pltpu.VMEM((SUBLANES - 1, CONV_CHUNK + CONV_HALO - SUBLANES, CONV_WIDTH), F32),
        ],
        compiler_params=pltpu.CompilerParams(
            dimension_semantics=("arbitrary",), vmem_limit_bytes=VMEM_LIMIT),
        name="mixer",
    )(x, meta, g1, w_in, pool_w, pool_scale, dw_w, dw_b, ln_g, ln_b, pw_w, pw_b, w_out, g2)


def _top16_pair(scores, sv_scr):
    cur = scores
    rank = jnp.full(scores.shape, float(PEER_TOPK), F32)
    for k in range(PEER_TOPK):
        m = jnp.max(cur, axis=1, keepdims=True)
        hit = cur == m
        rank = jnp.where(hit, float(k), rank)
        cur = jnp.where(hit, NEG_INF, cur)
        sv_scr[:, k:k + 1, :] = m
    return rank


def _route_tile(s_pair, sv_scr):
    ranks = _top16_pair(s_pair, sv_scr)
    r1, r2 = ranks[0], ranks[1]
    s1, s2 = s_pair[0], s_pair[1]
    sv1_scr, sv2_scr = sv_scr.at[0], sv_scr.at[1]
    sv2_lo = sv2_scr[0:SUBLANES, :]
    row = lax.broadcasted_iota(jnp.int32, (SUBLANES, LANES), 0)

    cands = [sv1_scr[0:1, :] + sv2_scr[...]]
    for a in range(1, SUBLANES):
        c = sv1_scr[a:a + 1, :] + sv2_lo
        cands.append(jnp.where(row < PEER_TOPK // (a + 1), c, NEG_INF))
    cands.append(sv1_scr[SUBLANES:PEER_TOPK, :] + sv2_scr[0:1, :])

    top = cands[0][0:1, :]
    cur = list(cands)
    thr = top
    for k in range(PEER_TOPK):
        m = cur[0][0:SUBLANES, :]
        m = jnp.maximum(m, cur[0][SUBLANES:, :])
        for c in cur[1:]:
            m = jnp.maximum(m, c)
        thr = jnp.max(m, axis=0, keepdims=True)
        cur = [jnp.where(c == thr, NEG_INF, c) for c in cur]

    zsum = jnp.zeros((1, LANES), F32)
    counts = []
    for a, c in enumerate(cands):
        sel = c >= thr
        e = jnp.where(sel, jnp.exp(c - top), 0.0)
        zsum = zsum + jnp.sum(e, axis=0, keepdims=True)
        if a < SUBLANES:
            counts.append(jnp.sum(sel.astype(F32), axis=0, keepdims=True))
        else:
            tail = sel.astype(F32)
            counts.extend(tail[b:b + 1, :] for b in range(SUBLANES))

    n = jnp.zeros(s1.shape, F32)
    for a in range(PEER_TOPK):
        n = jnp.where(r1 == float(a), counts[a], n)

    inv_z = 1.0 / zsum
    u = jnp.where(r1 < float(PEER_TOPK), jnp.exp(s1 - sv1_scr[0:1, :]), 0.0)
    v = jnp.where(r2 < float(PEER_TOPK), jnp.exp(s2 - sv2_scr[0:1, :]) * inv_z, 0.0)
    return u, v, n, r2


def _pack_bf16(x):
    return pltpu.bitcast(x.astype(BF16), jnp.uint32)


def _route_kernel(xn_ref, wqt_ref, keys_ref, u_ref, v_ref, n_ref, r2_ref,
                  qt_scr, s_scr, sv_scr):
    qt = jnp.dot(wqt_ref[...], xn_ref[...], preferred_element_type=F32)
    qt_scr[...] = qt.astype(BF16)

    def head(h, carry):
        for p in range(2):
            q0 = pl.multiple_of((2 * h + p) * PEER_HALF, PEER_HALF)
            s_scr[p] = jnp.dot(keys_ref[h, p], qt_scr[pl.ds(q0, PEER_HALF), :],
                               preferred_element_type=F32)
        for t in range(PEER_TILES):
            lanes = slice(t * LANES, (t + 1) * LANES)
            u, v, n, r2 = _route_tile(s_scr[:, :, lanes], sv_scr)
            u_ref[h, t] = u
            v_ref[h, t] = _pack_bf16(v)
            n_ref[h, t] = n
            r2_ref[h, t] = _pack_bf16(r2)
        return carry

    lax.fori_loop(0, PEER_HEADS, head, 0)


def _route_call(xn2, wq_t, keys):
    n_steps = SEQ // PEER_ROWS
    def fac(rows, dtype):
        shape = jax.ShapeDtypeStruct((PEER_HEADS, SEQ // LANES, rows, LANES), dtype)
        spec = pl.BlockSpec((PEER_HEADS, PEER_TILES, rows, LANES), lambda i: (0, i, 0, 0))
        return shape, spec

    facs = [fac(PEER_NKEYS, F32), fac(PEER_NKEYS // 2, jnp.uint32),
            fac(PEER_NKEYS, F32), fac(PEER_NKEYS // 2, jnp.uint32)]
    return pl.pallas_call(
        _route_kernel,
        grid=(n_steps,),
        in_specs=[
            pl.BlockSpec((D_MODEL, PEER_ROWS), lambda i: (0, i)),
            _const_spec((PEER_QW, D_MODEL)),
            _const_spec((PEER_HEADS, 2, PEER_NKEYS, PEER_HALF)),
        ],
        out_specs=[spec for _, spec in facs],
        out_shape=[shape for shape, _ in facs],
        scratch_shapes=[
            pltpu.VMEM((PEER_QW, PEER_ROWS), BF16),
            pltpu.VMEM((2, PEER_NKEYS, PEER_ROWS), F32),
            pltpu.VMEM((2, PEER_TOPK, LANES), F32),
        ],
        compiler_params=pltpu.CompilerParams(
            dimension_semantics=("arbitrary",), vmem_limit_bytes=VMEM_LIMIT),
        name="peer_route",
    )(xn2, wq_t, keys)


def _hidden_piece(p, kk, hid_ref, down_ref, xn_ref):
    toks = slice(p * MXU_PIECE, (p + 1) * MXU_PIECE)
    feat = slice(kk * HID_K_PIECE, (kk + 1) * HID_K_PIECE)
    part = jnp.dot(down_ref[:, feat], xn_ref[feat, toks], preferred_element_type=F32)
    if kk == 0:
        hid_ref[:, toks] = part
    else:
        hid_ref[:, toks] += part


def _up_piece(p, m, acc_ref, upt_ref, act_ref):
    toks = slice(p * MXU_PIECE, (p + 1) * MXU_PIECE)
    rows = slice(m * UP_M_PIECE, (m + 1) * UP_M_PIECE)
    acc_ref[rows, toks] += jnp.dot(upt_ref[rows, :], act_ref[:, toks],
                                   preferred_element_type=F32)


def _gate_tile(krow, il, t, hid_ref, act_ref, u_ref, n_ref, v_ref, r2_ref):
    erows = slice(il * PEER_NKEYS, (il + 1) * PEER_NKEYS)
    lanes = slice(t * LANES, (t + 1) * LANES)
    tile = (PEER_NKEYS, LANES)
    w = jnp.zeros(tile, BF16)
    for h in range(PEER_HEADS):
        ui = jnp.broadcast_to(u_ref[h, t, krow:krow + 1, :].astype(BF16), tile)
        ni = jnp.broadcast_to(n_ref[h, t, krow:krow + 1, :].astype(BF16), tile)
        vt = pltpu.bitcast(v_ref[h, t], BF16)
        rt = pltpu.bitcast(r2_ref[h, t], BF16)
        w = w + jnp.where(rt < ni, vt * ui, jnp.zeros((), BF16))
    hid = hid_ref[erows, lanes]
    gelu = 0.5 * hid * (1.0 + lax.erf(hid * (1.0 / math.sqrt(2.0))))
    act_ref[erows, lanes] = w * gelu.astype(BF16)


def _pipeline_part(half, hid_cur, act_cur, hid_next, down_ref, xn_ref,
                   act_prev, upt_ref, acc_ref, u_ref, n_ref, v_ref, r2_ref):
    tiles = [(il, t) for il in range(HALF_KEYS) for t in range(PEER_TILES)]
    n_tok = PEER_ROWS // MXU_PIECE
    ups = [functools.partial(_up_piece, p, m, acc_ref, upt_ref, act_prev)
           for p in range(n_tok) for m in range(D_MODEL // UP_M_PIECE)]
    hids = [functools.partial(_hidden_piece, p, kk, hid_next, down_ref, xn_ref)
            for p in range(n_tok) for kk in range(D_MODEL // HID_K_PIECE)]
    for k, (il, t) in enumerate(tiles):
        for pieces in (ups, hids):
            lo = -(-k * len(pieces) // len(tiles))
            hi = -(-(k + 1) * len(pieces) // len(tiles))
            for piece in pieces[lo:hi]:
                piece()
        _gate_tile(half * HALF_KEYS + il, il, t, hid_cur, act_cur,
                   u_ref, n_ref, v_ref, r2_ref)


def _expert_kernel(xna_ref, xnb_ref, dn0_ref, dna_ref, dnb_ref, upa_ref, upb_ref,
                   u_ref, n_ref, v_ref, r2_ref, h1_ref, gf_ref, out_ref,
                   hid0, hid1, act0, act1, acc):
    s = pl.program_id(0)

    @pl.when(s == 0)
    def _():
        for p in range(PEER_ROWS // MXU_PIECE):
            for kk in range(D_MODEL // HID_K_PIECE):
                _hidden_piece(p, kk, hid0, dn0_ref, xna_ref)
        act1[...] = jnp.zeros(act1.shape, BF16)
        acc[...] = jnp.zeros(acc.shape, F32)

    _pipeline_part(0, hid0, act0, hid1, dna_ref, xna_ref, act1, upa_ref, acc,
                   u_ref, n_ref, v_ref, r2_ref)

    @pl.when(s % STEPS_PER_BLOCK == 0)
    def _():
        @pl.when(s > 0)
        def _():
            h2 = h1_ref[...] + acc[...].T
            ms = jnp.mean(h2 * h2, axis=-1, keepdims=True)
            out_ref[...] = h2 * lax.rsqrt(ms + RMS_EPS) * gf_ref[...]

        acc[...] = jnp.zeros(acc.shape, F32)

    _pipeline_part(1, hid1, act1, hid0, dnb_ref, xnb_ref, act0, upb_ref, acc,
                   u_ref, n_ref, v_ref, r2_ref)


def _expert_call(xn2, down, up_t, u, v, n, r2, h1, gf):
    n_blocks = SEQ // PEER_ROWS
    n_steps = n_blocks * STEPS_PER_BLOCK + 1
    last = n_blocks - 1
    halves = PEER_EXPERTS // HALF_CHUNK

    def tok(f):
        return lambda s: (jnp.minimum(f(s) // halves, last), 0)

    xn_spec = lambda f: pl.BlockSpec((D_MODEL, PEER_ROWS), lambda s: tok(f)(s)[::-1])
    dn_spec = lambda f: pl.BlockSpec((HALF_CHUNK, D_MODEL), lambda s: (f(s) % halves, 0))
    up_spec = lambda f: pl.BlockSpec((None, D_MODEL, HALF_CHUNK),
                                     lambda s: (f(s) % halves, 0, 0))
    key_spec = pl.BlockSpec(
        (PEER_HEADS, PEER_TILES, 2 * HALF_KEYS, LANES),
        lambda s: (0, jnp.minimum(s // STEPS_PER_BLOCK, last), s % STEPS_PER_BLOCK, 0))
    full_spec = pl.BlockSpec(
        (PEER_HEADS, PEER_TILES, PEER_NKEYS // 2, LANES),
        lambda s: (0, jnp.minimum(s // STEPS_PER_BLOCK, last), 0, 0))
    done = lambda s: jnp.maximum(2 * s - 1, 0)
    return pl.pallas_call(
        _expert_kernel,
        grid=(n_steps,),
        in_specs=[
            xn_spec(lambda s: 2 * s + 1),
            xn_spec(lambda s: 2 * s + 2),
            pl.BlockSpec((HALF_CHUNK, D_MODEL), lambda s: (0, 0), pipeline_mode=pl.Buffered(1)),
            dn_spec(lambda s: 2 * s + 1),
            dn_spec(lambda s: 2 * s + 2),
            up_spec(lambda s: 2 * s + halves - 1),
            up_spec(lambda s: 2 * s),
            key_spec, key_spec, full_spec, full_spec,
            pl.BlockSpec((PEER_ROWS, D_MODEL), tok(done), pipeline_mode=pl.Buffered(1)),
            pl.BlockSpec((1, D_MODEL), lambda s: (0, 0)),
        ],
        out_specs=pl.BlockSpec((PEER_ROWS, D_MODEL), tok(done)),
        out_shape=jax.ShapeDtypeStruct((SEQ, D_MODEL), F32),
        scratch_shapes=[
            pltpu.VMEM((HALF_CHUNK, PEER_ROWS), F32),
            pltpu.VMEM((HALF_CHUNK, PEER_ROWS), F32),
            pltpu.VMEM((HALF_CHUNK, PEER_ROWS), BF16),
            pltpu.VMEM((HALF_CHUNK, PEER_ROWS), BF16),
            pltpu.VMEM((D_MODEL, PEER_ROWS), F32),
        ],
        compiler_params=pltpu.CompilerParams(
            dimension_semantics=("arbitrary",), vmem_limit_bytes=VMEM_LIMIT),
        name="peer_experts",
    )(xn2, xn2, down, down, down, up_t, up_t, u, n, v, r2, h1, gf)


def kernel(x, meta_tokens, mix_norm_g, w_in, pool_w, pool_scale, conv_dw_w, conv_dw_b, conv_ln_g, conv_ln_b, conv_pw_w, conv_pw_b, w_out, ffn_norm_g, peer_wq, peer_keys, peer_down, peer_up, final_norm_g):
    assert x.shape == (1, SEQ, D_MODEL) and w_in.shape[0] == 1
    row = lambda a: a.reshape(1, -1).astype(F32)
    h1, xn2 = _mixer_call(
        x[0], meta_tokens, row(mix_norm_g[0]), w_in[0].astype(BF16),
        pool_w[0].astype(BF16), row(pool_scale[0]), conv_dw_w[0], row(conv_dw_b[0]),
        row(conv_ln_g[0]), row(conv_ln_b[0]), conv_pw_w[0].astype(BF16),
        row(conv_pw_b[0]), w_out[0].astype(BF16), row(ffn_norm_g[0]))
    u, v, n, r2 = _route_call(xn2, peer_wq[0].T.astype(BF16), peer_keys[0].astype(BF16))
    up_t = peer_up[0].astype(BF16).reshape(-1, HALF_CHUNK, D_MODEL).transpose(0, 2, 1)
    out = _expert_call(xn2, peer_down[0].astype(BF16), up_t,
                       u, v, n, r2, h1, row(final_norm_g))
    return out[None]
```

```python
import functools
import math

import jax
import jax.numpy as jnp
from jax import lax
from jax.experimental import pallas as pl
from jax.experimental.pallas import tpu as pltpu

F32 = jnp.float32
BF16 = jnp.bfloat16

D_MODEL = 2048
SEQ = 8192
N_META = 16
POOL_WIDTH = 1024
CONV_WIDTH = 1024
IN_WIDTH = POOL_WIDTH + 2 * CONV_WIDTH
POOL_WINDOWS = (2, 4, 8, 16)
POOL_GROUP_DIM = 256
CONV_KSIZE = 31
PEER_HEADS = 8
PEER_NKEYS = 128
PEER_EXPERTS = PEER_NKEYS * PEER_NKEYS
PEER_HALF = 128
PEER_QW = PEER_HEADS * 2 * PEER_HALF
PEER_TOPK = 16
RMS_EPS = 1e-6
LN_EPS = 1e-5

LANES = 128
SUBLANES = 8
VMEM_LIMIT = 58 * 1024 * 1024

MIX_ROWS = 256
POOL_HALO = 16
CONV_HALO = 32
CONV_CHUNK = 128
PEER_ROWS = 512
PEER_TILES = PEER_ROWS // LANES
HALF_CHUNK = 512
HALF_KEYS = HALF_CHUNK // PEER_NKEYS
STEPS_PER_BLOCK = PEER_EXPERTS // (2 * HALF_CHUNK)
MXU_PIECE = 256
HID_K_PIECE = 512
UP_M_PIECE = 256
NEG_INF = float("-inf")


def _const_spec(shape):
    zeros = (0,) * len(shape)
    return pl.BlockSpec(shape, lambda *_: zeros, pipeline_mode=pl.Buffered(1))


def _mixer_kernel(x_ref, meta_ref, g1_ref, win_ref, poolw_ref, pscale_ref,
                  dww_ref, dwb_ref, lng_ref, lnb_ref, pww_ref, pwb_ref,
                  wout_ref, g2_ref, up_ref,
                  h1_ref, xn2_ref, upb_ref,
                  xp_ext, glu_ext, cv_scr, ycat_scr, shift_scr):
    step = pl.program_id(0)
    rows = MIX_ROWS

    upb_ref[...] = up_ref[...].astype(BF16)

    def project(tok):
        ms = jnp.mean(tok * tok, axis=-1, keepdims=True)
        hn = tok * lax.rsqrt(ms + RMS_EPS) * g1_ref[...]
        return jnp.dot(hn.astype(BF16), win_ref[...], preferred_element_type=F32)

    def glu_of(proj):
        xa = proj[:, POOL_WIDTH:POOL_WIDTH + CONV_WIDTH]
        xg = proj[:, POOL_WIDTH + CONV_WIDTH:]
        return xa * jax.nn.sigmoid(xg)

    @pl.when(step == 0)
    def _():
        pm = project(meta_ref[...])
        xp_ext[0:POOL_HALO, :] = pm[:, :POOL_WIDTH]
        glu_ext[0:CONV_HALO - N_META, :] = jnp.zeros((CONV_HALO - N_META, CONV_WIDTH), F32)
        glu_ext[CONV_HALO - N_META:CONV_HALO, :] = glu_of(pm)

    @pl.when(step > 0)
    def _():
        xp_ext[0:POOL_HALO, :] = xp_ext[rows:rows + POOL_HALO, :]
        glu_ext[0:CONV_HALO, :] = glu_ext[rows:rows + CONV_HALO, :]

    x = x_ref[...]
    proj = project(x)
    xp_ext[POOL_HALO:POOL_HALO + rows, :] = proj[:, :POOL_WIDTH]
    glu_ext[CONV_HALO:CONV_HALO + rows, :] = glu_of(proj)

    for g, win in enumerate(POOL_WINDOWS):
        c0 = g * POOL_GROUP_DIM
        cols = slice(c0, c0 + POOL_GROUP_DIM)
        cur = xp_ext[POOL_HALO:POOL_HALO + rows, cols]
        acc = cur
        for k in range(1, win):
            acc = acc + xp_ext[POOL_HALO - k:POOL_HALO - k + rows, cols]
        pooled = acc * (1.0 / win) - cur
        ya = jnp.dot(pooled.astype(BF16), poolw_ref[g], preferred_element_type=F32)
        ycat_scr[:, cols] = (ya * pscale_ref[:, cols]).astype(BF16)

    span = CONV_CHUNK + CONV_HALO - SUBLANES

    def conv_chunk(r, carry):
        r0 = pl.multiple_of(r * CONV_CHUNK, CONV_CHUNK)
        window = glu_ext[pl.ds(r0, CONV_CHUNK + CONV_HALO), :]
        for b in range(1, SUBLANES):
            shift_scr[b - 1] = window[b:b + span, :]
        for lt in range(CONV_WIDTH // LANES):
            lanes = slice(lt * LANES, (lt + 1) * LANES)
            acc = jnp.broadcast_to(dwb_ref[:, lanes], (CONV_CHUNK, LANES))
            for k in range(CONV_KSIZE):
                a, b = divmod(CONV_HALO - (CONV_KSIZE - 1) + k, SUBLANES)
                if b == 0:
                    src = glu_ext[pl.ds(pl.multiple_of(r0 + a * SUBLANES, SUBLANES), CONV_CHUNK), lanes]
                else:
                    src = shift_scr[b - 1, a * SUBLANES:a * SUBLANES + CONV_CHUNK, lanes]
                acc = acc + dww_ref[k:k + 1, lanes] * src
            cv_scr[pl.ds(r0, CONV_CHUNK), lanes] = acc
        return carry

    lax.fori_loop(0, rows // CONV_CHUNK, conv_chunk, 0)

    cv = cv_scr[...]
    mu = jnp.mean(cv, axis=-1, keepdims=True)
    xc = cv - mu
    var = jnp.mean(xc * xc, axis=-1, keepdims=True)
    ln = xc * lax.rsqrt(var + LN_EPS) * lng_ref[...] + lnb_ref[...]
    sw = ln * jax.nn.sigmoid(ln)
    yb = jnp.dot(sw.astype(BF16), pww_ref[...], preferred_element_type=F32) + pwb_ref[...]
    ycat_scr[:, POOL_WIDTH:] = yb.astype(BF16)

    h1 = x + jnp.dot(ycat_scr[...], wout_ref[...], preferred_element_type=F32)
    h1_ref[...] = h1
    ms = jnp.mean(h1 * h1, axis=-1, keepdims=True)
    xn2_ref[...] = (h1 * lax.rsqrt(ms + RMS_EPS) * g2_ref[...]).T.astype(BF16)


def _mixer_call(x, meta, g1, w_in, pool_w, pool_scale, dw_w, dw_b, ln_g, ln_b,
                pw_w, pw_b, w_out, g2, up):
    n_steps = SEQ // MIX_ROWS
    row_spec = pl.BlockSpec((MIX_ROWS, D_MODEL), lambda i: (i, 0))
    table_spec = pl.BlockSpec((PEER_EXPERTS // n_steps, D_MODEL), lambda i: (i, 0))
    return pl.pallas_call(
        _mixer_kernel,
        grid=(n_steps,),
        in_specs=[
            row_spec,
            _const_spec((N_META, D_MODEL)),
            _const_spec((1, D_MODEL)),
            _const_spec((D_MODEL, IN_WIDTH)),
            _const_spec((len(POOL_WINDOWS), POOL_GROUP_DIM, POOL_GROUP_DIM)),
            _const_spec((1, POOL_WIDTH)),
            _const_spec((CONV_KSIZE, CONV_WIDTH)),
            _const_spec((1, CONV_WIDTH)),
            _const_spec((1, CONV_WIDTH)),
            _const_spec((1, CONV_WIDTH)),
            _const_spec((CONV_WIDTH, CONV_WIDTH)),
            _const_spec((1, CONV_WIDTH)),
            _const_spec((D_MODEL, D_MODEL)),
            _const_spec((1, D_MODEL)),
            table_spec,
        ],
        out_specs=[row_spec, pl.BlockSpec((D_MODEL, MIX_ROWS), lambda i: (0, i)), table_spec],
        out_shape=[jax.ShapeDtypeStruct((SEQ, D_MODEL), F32),
                   jax.ShapeDtypeStruct((D_MODEL, SEQ), BF16),
                   jax.ShapeDtypeStruct((PEER_EXPERTS, D_MODEL), BF16)],
        scratch_shapes=[
            pltpu.VMEM((POOL_HALO + MIX_ROWS, POOL_WIDTH), F32),
            pltpu.VMEM((CONV_HALO + MIX_ROWS, CONV_WIDTH), F32),
            pltpu.VMEM((MIX_ROWS, CONV_WIDTH), F32),
            pltpu.VMEM((MIX_ROWS, D_MODEL), BF16),
            pltpu.VMEM((SUBLANES - 1, CONV_CHUNK + CONV_HALO - SUBLANES, CONV_WIDTH), F32),
        ],
        compiler_params=pltpu.CompilerParams(
            dimension_semantics=("arbitrary",), vmem_limit_bytes=VMEM_LIMIT),
        name="mixer",
    )(x, meta, g1, w_in, pool_w, pool_scale, dw_w, dw_b, ln_g, ln_b, pw_w, pw_b, w_out, g2, up)


def _top16_pair(scores, sv_scr):
    cur = scores
    rank = jnp.full(scores.shape, float(PEER_TOPK), F32)
    for k in range(PEER_TOPK):
        m = jnp.max(cur, axis=1, keepdims=True)
        hit = cur == m
        rank = jnp.where(hit, float(k), rank)
        cur = jnp.where(hit, NEG_INF, cur)
        sv_scr[:, k:k + 1, :] = m
    return rank


def _route_tile(s_pair, sv_scr):
    ranks = _top16_pair(s_pair, sv_scr)
    r1, r2 = ranks[0], ranks[1]
    s1, s2 = s_pair[0], s_pair[1]
    sv1_scr, sv2_scr = sv_scr.at[0], sv_scr.at[1]
    sv2_lo = sv2_scr[0:SUBLANES, :]
    row = lax.broadcasted_iota(jnp.int32, (SUBLANES, LANES), 0)

    cands = [sv1_scr[0:1, :] + sv2_scr[...]]
    for a in range(1, SUBLANES):
        c = sv1_scr[a:a + 1, :] + sv2_lo
        cands.append(jnp.where(row < PEER_TOPK // (a + 1), c, NEG_INF))
    cands.append(sv1_scr[SUBLANES:PEER_TOPK, :] + sv2_scr[0:1, :])

    top = cands[0][0:1, :]
    cur = list(cands)
    thr = top
    for k in range(PEER_TOPK):
        m = cur[0][0:SUBLANES, :]
        m = jnp.maximum(m, cur[0][SUBLANES:, :])
        for c in cur[1:]:
            m = jnp.maximum(m, c)
        thr = jnp.max(m, axis=0, keepdims=True)
        cur = [jnp.where(c == thr, NEG_INF, c) for c in cur]

    zsum = jnp.zeros((1, LANES), F32)
    counts = []
    for a, c in enumerate(cands):
        sel = c >= thr
        e = jnp.where(sel, jnp.exp(c - top), 0.0)
        zsum = zsum + jnp.sum(e, axis=0, keepdims=True)
        if a < SUBLANES:
            counts.append(jnp.sum(sel.astype(F32), axis=0, keepdims=True))
        else:
            tail = sel.astype(F32)
            counts.extend(tail[b:b + 1, :] for b in range(SUBLANES))

    n = jnp.zeros(s1.shape, F32)
    for a in range(PEER_TOPK):
        n = jnp.where(r1 == float(a), counts[a], n)

    inv_z = 1.0 / zsum
    u = jnp.where(r1 < float(PEER_TOPK), jnp.exp(s1 - sv1_scr[0:1, :]), 0.0)
    v = jnp.where(r2 < float(PEER_TOPK), jnp.exp(s2 - sv2_scr[0:1, :]) * inv_z, 0.0)
    return u, v, n, r2


def _pack_bf16(x):
    return pltpu.bitcast(x.astype(BF16), jnp.uint32)


def _route_kernel(xn_ref, wqt_ref, keys_ref, down_ref,
                  u_ref, v_ref, n_ref, r2_ref, downb_ref,
                  qt_scr, s_scr, sv_scr):
    downb_ref[...] = down_ref[...].astype(BF16)

    qt = jnp.dot(wqt_ref[...], xn_ref[...], preferred_element_type=F32)
    qt_scr[...] = qt.astype(BF16)

    def head(h, carry):
        for p in range(2):
            q0 = pl.multiple_of((2 * h + p) * PEER_HALF, PEER_HALF)
            s_scr[p] = jnp.dot(keys_ref[h, p], qt_scr[pl.ds(q0, PEER_HALF), :],
                               preferred_element_type=F32)
        for t in range(PEER_TILES):
            lanes = slice(t * LANES, (t + 1) * LANES)
            u, v, n, r2 = _route_tile(s_scr[:, :, lanes], sv_scr)
            u_ref[h, t] = u
            v_ref[h, t] = _pack_bf16(v)
            n_ref[h, t] = n
            r2_ref[h, t] = _pack_bf16(r2)
        return carry

    lax.fori_loop(0, PEER_HEADS, head, 0)


def _route_call(xn2, wq_t, keys, down):
    n_steps = SEQ // PEER_ROWS
    def fac(rows, dtype):
        shape = jax.ShapeDtypeStruct((PEER_HEADS, SEQ // LANES, rows, LANES), dtype)
        spec = pl.BlockSpec((PEER_HEADS, PEER_TILES, rows, LANES), lambda i: (0, i, 0, 0))
        return shape, spec

    facs = [fac(PEER_NKEYS, F32), fac(PEER_NKEYS // 2, jnp.uint32),
            fac(PEER_NKEYS, F32), fac(PEER_NKEYS // 2, jnp.uint32)]
    table_spec = pl.BlockSpec((PEER_EXPERTS // n_steps, D_MODEL), lambda i: (i, 0))
    return pl.pallas_call(
        _route_kernel,
        grid=(n_steps,),
        in_specs=[
            pl.BlockSpec((D_MODEL, PEER_ROWS), lambda i: (0, i)),
            _const_spec((PEER_QW, D_MODEL)),
            _const_spec((PEER_HEADS, 2, PEER_NKEYS, PEER_HALF)),
            table_spec,
        ],
        out_specs=[spec for _, spec in facs] + [table_spec],
        out_shape=[shape for shape, _ in facs]
        + [jax.ShapeDtypeStruct((PEER_EXPERTS, D_MODEL), BF16)],
        scratch_shapes=[
            pltpu.VMEM((PEER_QW, PEER_ROWS), BF16),
            pltpu.VMEM((2, PEER_NKEYS, PEER_ROWS), F32),
            pltpu.VMEM((2, PEER_TOPK, LANES), F32),
        ],
        compiler_params=pltpu.CompilerParams(
            dimension_semantics=("arbitrary",), vmem_limit_bytes=VMEM_LIMIT),
        name="peer_route",
    )(xn2, wq_t, keys, down)


def _hidden_piece(p, kk, hid_ref, down_ref, xn_ref):
    toks = slice(p * MXU_PIECE, (p + 1) * MXU_PIECE)
    feat = slice(kk * HID_K_PIECE, (kk + 1) * HID_K_PIECE)
    part = jnp.dot(down_ref[:, feat], xn_ref[feat, toks], preferred_element_type=F32)
    if kk == 0:
        hid_ref[:, toks] = part
    else:
        hid_ref[:, toks] += part


def _up_piece(p, m, acc_ref, upt_ref, act_ref):
    toks = slice(p * MXU_PIECE, (p + 1) * MXU_PIECE)
    rows = slice(m * UP_M_PIECE, (m + 1) * UP_M_PIECE)
    acc_ref[rows, toks] += jnp.dot(upt_ref[rows, :], act_ref[:, toks],
                                   preferred_element_type=F32)


def _gate_tile(krow, il, t, hid_ref, act_ref, u_ref, n_ref, v_ref, r2_ref):
    erows = slice(il * PEER_NKEYS, (il + 1) * PEER_NKEYS)
    lanes = slice(t * LANES, (t + 1) * LANES)
    tile = (PEER_NKEYS, LANES)
    w = jnp.zeros(tile, BF16)
    for h in range(PEER_HEADS):
        ui = jnp.broadcast_to(u_ref[h, t, krow:krow + 1, :].astype(BF16), tile)
        ni = jnp.broadcast_to(n_ref[h, t, krow:krow + 1, :].astype(BF16), tile)
        vt = pltpu.bitcast(v_ref[h, t], BF16)
        rt = pltpu.bitcast(r2_ref[h, t], BF16)
        w = w + jnp.where(rt < ni, vt * ui, jnp.zeros((), BF16))
    hid = hid_ref[erows, lanes]
    gelu = 0.5 * hid * (1.0 + lax.erf(hid * (1.0 / math.sqrt(2.0))))
    act_ref[erows, lanes] = w * gelu.astype(BF16)


def _pipeline_part(krow0, hid_cur, act_cur, hid_next, down_ref, xn_ref,
                   act_prev, upt_ref, acc_ref, u_ref, n_ref, v_ref, r2_ref):
    tiles = [(il, t) for il in range(HALF_KEYS) for t in range(PEER_TILES)]
    n_tok = PEER_ROWS // MXU_PIECE
    ups = [functools.partial(_up_piece, p, m, acc_ref, upt_ref, act_prev)
           for p in range(n_tok) for m in range(D_MODEL // UP_M_PIECE)]
    hids = [functools.partial(_hidden_piece, p, kk, hid_next, down_ref, xn_ref)
            for p in range(n_tok) for kk in range(D_MODEL // HID_K_PIECE)]
    for k, (il, t) in enumerate(tiles):
        for pieces in (ups, hids):
            lo = -(-k * len(pieces) // len(tiles))
            hi = -(-(k + 1) * len(pieces) // len(tiles))
            for piece in pieces[lo:hi]:
                piece()
        _gate_tile(krow0 + il, il, t, hid_cur, act_cur, u_ref, n_ref, v_ref, r2_ref)


def _expert_kernel(xn_ref, dn_ref, up_ref, ua_ref, na_ref, va_ref, r2a_ref,
                   ub_ref, nb_ref, vb_ref, r2b_ref, h1_ref, gf_ref, out_ref,
                   hid0, hid1, act0, act1, acc):
    s = pl.program_id(0)
    lo, hi = pl.ds(0, HALF_CHUNK), pl.ds(HALF_CHUNK, HALF_CHUNK)

    @pl.when(s == 0)
    def _():
        hid1[...] = jnp.zeros(hid1.shape, F32)
        act0[...] = jnp.zeros(act0.shape, BF16)
        act1[...] = jnp.zeros(act1.shape, BF16)
        acc[...] = jnp.zeros(acc.shape, F32)

    _pipeline_part(HALF_KEYS, hid1, act1, hid0, dn_ref.at[lo, :], xn_ref,
                   act0, up_ref.at[:, lo], acc, ua_ref, na_ref, va_ref, r2a_ref)

    @pl.when(s == 0)
    def _():
        act0[...] = jnp.zeros(act0.shape, BF16)

    _pipeline_part(0, hid0, act0, hid1, dn_ref.at[hi, :], xn_ref,
                   act1, up_ref.at[:, hi], acc, ub_ref, nb_ref, vb_ref, r2b_ref)

    @pl.when(jnp.logical_and(s % STEPS_PER_BLOCK == 0, s > 0))
    def _():
        h2 = h1_ref[...] + acc[...].T
        ms = jnp.mean(h2 * h2, axis=-1, keepdims=True)
        out_ref[...] = h2 * lax.rsqrt(ms + RMS_EPS) * gf_ref[...]
        acc[...] = jnp.zeros(acc.shape, F32)


def _expert_call(xn2t, down, up_t, u, v, n, r2, h1, gf):
    n_blocks = SEQ // PEER_ROWS
    n_steps = n_blocks * STEPS_PER_BLOCK + 1
    last = n_blocks - 1
    halves = PEER_EXPERTS // HALF_CHUNK

    def blk(f):
        return lambda s: jnp.clip(f(s) // halves, 0, last)

    prev_half = lambda s: 2 * s - 1
    this_half = lambda s: 2 * s

    def key_spec(f):
        return pl.BlockSpec((PEER_HEADS, PEER_TILES, 2 * HALF_KEYS, LANES),
                            lambda s: (0, blk(f)(s), (f(s) % halves) // 2, 0))

    def full_spec(f):
        return pl.BlockSpec((PEER_HEADS, PEER_TILES, PEER_NKEYS // 2, LANES),
                            lambda s: (0, blk(f)(s), 0, 0))

    done_blk = lambda s: (jnp.maximum(s - 1, 0) // STEPS_PER_BLOCK, 0)
    return pl.pallas_call(
        _expert_kernel,
        grid=(n_steps,),
        in_specs=[
            pl.BlockSpec((D_MODEL, PEER_ROWS), lambda s: (0, blk(this_half)(s))),
            pl.BlockSpec((2 * HALF_CHUNK, D_MODEL), lambda s: (s % STEPS_PER_BLOCK, 0)),
            pl.BlockSpec((None, D_MODEL, 2 * HALF_CHUNK),
                         lambda s: ((s + STEPS_PER_BLOCK - 1) % STEPS_PER_BLOCK, 0, 0)),
            key_spec(prev_half), key_spec(prev_half), full_spec(prev_half), full_spec(prev_half),
            key_spec(this_half), key_spec(this_half), full_spec(this_half), full_spec(this_half),
            pl.BlockSpec((PEER_ROWS, D_MODEL), done_blk, pipeline_mode=pl.Buffered(1)),
            pl.BlockSpec((1, D_MODEL), lambda s: (0, 0)),
        ],
        out_specs=pl.BlockSpec((PEER_ROWS, D_MODEL), done_blk),
        out_shape=jax.ShapeDtypeStruct((SEQ, D_MODEL), F32),
        scratch_shapes=[
            pltpu.VMEM((HALF_CHUNK, PEER_ROWS), F32),
            pltpu.VMEM((HALF_CHUNK, PEER_ROWS), F32),
            pltpu.VMEM((HALF_CHUNK, PEER_ROWS), BF16),
            pltpu.VMEM((HALF_CHUNK, PEER_ROWS), BF16),
            pltpu.VMEM((D_MODEL, PEER_ROWS), F32),
        ],
        compiler_params=pltpu.CompilerParams(
            dimension_semantics=("arbitrary",), vmem_limit_bytes=VMEM_LIMIT),
        name="peer_experts",
    )(xn2t, down, up_t, u, n, v, r2, u, n, v, r2, h1, gf)


def kernel(x, meta_tokens, mix_norm_g, w_in, pool_w, pool_scale, conv_dw_w, conv_dw_b, conv_ln_g, conv_ln_b, conv_pw_w, conv_pw_b, w_out, ffn_norm_g, peer_wq, peer_keys, peer_down, peer_up, final_norm_g):
    assert x.shape == (1, SEQ, D_MODEL) and w_in.shape[0] == 1
    row = lambda a: a.reshape(1, -1).astype(F32)
    h1, xn2, up_b = _mixer_call(
        x[0], meta_tokens, row(mix_norm_g[0]), w_in[0].astype(BF16),
        pool_w[0].astype(BF16), row(pool_scale[0]), conv_dw_w[0], row(conv_dw_b[0]),
        row(conv_ln_g[0]), row(conv_ln_b[0]), conv_pw_w[0].astype(BF16),
        row(conv_pw_b[0]), w_out[0].astype(BF16), row(ffn_norm_g[0]), peer_up[0])
    u, v, n, r2, down_b = _route_call(xn2, peer_wq[0].T.astype(BF16),
                                      peer_keys[0].astype(BF16), peer_down[0])
    up_t = up_b.reshape(-1, 2 * HALF_CHUNK, D_MODEL).transpose(0, 2, 1)
    out = _expert_call(xn2, down_b, up_t, u, v, n, r2, h1, row(final_norm_g))
    return out[None]
```

```python
import functools
import math

import jax
import jax.numpy as jnp
from jax import lax
from jax.experimental import pallas as pl
from jax.experimental.pallas import tpu as pltpu

F32 = jnp.float32
BF16 = jnp.bfloat16

D_MODEL = 2048
SEQ = 8192
N_META = 16
POOL_WIDTH = 1024
CONV_WIDTH = 1024
IN_WIDTH = POOL_WIDTH + 2 * CONV_WIDTH
POOL_WINDOWS = (2, 4, 8, 16)
POOL_GROUP_DIM = 256
CONV_KSIZE = 31
PEER_HEADS = 8
PEER_NKEYS = 128
PEER_EXPERTS = PEER_NKEYS * PEER_NKEYS
PEER_HALF = 128
PEER_QW = PEER_HEADS * 2 * PEER_HALF
PEER_TOPK = 16
RMS_EPS = 1e-6
LN_EPS = 1e-5

LANES = 128
SUBLANES = 8
VMEM_LIMIT = 58 * 1024 * 1024

MIX_ROWS = 256
POOL_HALO = 16
CONV_HALO = 32
CONV_CHUNK = 128
PEER_ROWS = 512
PEER_TILES = PEER_ROWS // LANES
HALF_CHUNK = 512
HALF_KEYS = HALF_CHUNK // PEER_NKEYS
STEPS_PER_BLOCK = PEER_EXPERTS // (2 * HALF_CHUNK)
MXU_PIECE = 256
HID_K_PIECE = 512
UP_M_PIECE = 256
NEG_INF = float("-inf")
_RANK_TAG_STEP = 2.0 ** 104
_RANK_TAGS = [-((2.0 ** 24 - 16 + k) * _RANK_TAG_STEP) for k in range(PEER_TOPK)]


def _const_spec(shape):
    zeros = (0,) * len(shape)
    return pl.BlockSpec(shape, lambda *_: zeros, pipeline_mode=pl.Buffered(1))


def _mixer_kernel(x_ref, meta_ref, g1_ref, win_ref, poolw_ref, pscale_ref,
                  dww_ref, dwb_ref, lng_ref, lnb_ref, pww_ref, pwb_ref,
                  wout_ref, g2_ref, up_ref,
                  h1_ref, xn2_ref, upb_ref,
                  xp_ext, glu_ext, cv_scr, ycat_scr, shift_scr):
    step = pl.program_id(0)
    rows = MIX_ROWS

    upb_ref[...] = up_ref[...].astype(BF16)

    def project(tok):
        ms = jnp.mean(tok * tok, axis=-1, keepdims=True)
        hn = tok * lax.rsqrt(ms + RMS_EPS) * g1_ref[...]
        return jnp.dot(hn.astype(BF16), win_ref[...], preferred_element_type=F32)

    def glu_of(proj):
        xa = proj[:, POOL_WIDTH:POOL_WIDTH + CONV_WIDTH]
        xg = proj[:, POOL_WIDTH + CONV_WIDTH:]
        return xa * jax.nn.sigmoid(xg)

    @pl.when(step == 0)
    def _():
        pm = project(meta_ref[...])
        xp_ext[0:POOL_HALO, :] = pm[:, :POOL_WIDTH]
        glu_ext[0:CONV_HALO - N_META, :] = jnp.zeros((CONV_HALO - N_META, CONV_WIDTH), F32)
        glu_ext[CONV_HALO - N_META:CONV_HALO, :] = glu_of(pm)

    @pl.when(step > 0)
    def _():
        xp_ext[0:POOL_HALO, :] = xp_ext[rows:rows + POOL_HALO, :]
        glu_ext[0:CONV_HALO, :] = glu_ext[rows:rows + CONV_HALO, :]

    x = x_ref[...]
    proj = project(x)
    xp_ext[POOL_HALO:POOL_HALO + rows, :] = proj[:, :POOL_WIDTH]
    glu_ext[CONV_HALO:CONV_HALO + rows, :] = glu_of(proj)

    for g, win in enumerate(POOL_WINDOWS):
        c0 = g * POOL_GROUP_DIM
        cols = slice(c0, c0 + POOL_GROUP_DIM)
        cur = xp_ext[POOL_HALO:POOL_HALO + rows, cols]
        acc = cur
        for k in range(1, win):
            acc = acc + xp_ext[POOL_HALO - k:POOL_HALO - k + rows, cols]
        pooled = acc * (1.0 / win) - cur
        ya = jnp.dot(pooled.astype(BF16), poolw_ref[g], preferred_element_type=F32)
        ycat_scr[:, cols] = (ya * pscale_ref[:, cols]).astype(BF16)

    span = CONV_CHUNK + CONV_HALO - SUBLANES

    def conv_chunk(r, carry):
        r0 = pl.multiple_of(r * CONV_CHUNK, CONV_CHUNK)
        window = glu_ext[pl.ds(r0, CONV_CHUNK + CONV_HALO), :]
        for b in range(1, SUBLANES):
            shift_scr[b - 1] = window[b:b + span, :]
        for lt in range(CONV_WIDTH // LANES):
            lanes = slice(lt * LANES, (lt + 1) * LANES)
            acc = jnp.broadcast_to(dwb_ref[:, lanes], (CONV_CHUNK, LANES))
            for k in range(CONV_KSIZE):
                a, b = divmod(CONV_HALO - (CONV_KSIZE - 1) + k, SUBLANES)
                if b == 0:
                    src = glu_ext[pl.ds(pl.multiple_of(r0 + a * SUBLANES, SUBLANES), CONV_CHUNK), lanes]
                else:
                    src = shift_scr[b - 1, a * SUBLANES:a * SUBLANES + CONV_CHUNK, lanes]
                acc = acc + dww_ref[k:k + 1, lanes] * src
            cv_scr[pl.ds(r0, CONV_CHUNK), lanes] = acc
        return carry

    lax.fori_loop(0, rows // CONV_CHUNK, conv_chunk, 0)

    cv = cv_scr[...]
    mu = jnp.mean(cv, axis=-1, keepdims=True)
    xc = cv - mu
    var = jnp.mean(xc * xc, axis=-1, keepdims=True)
    ln = xc * lax.rsqrt(var + LN_EPS) * lng_ref[...] + lnb_ref[...]
    sw = ln * jax.nn.sigmoid(ln)
    yb = jnp.dot(sw.astype(BF16), pww_ref[...], preferred_element_type=F32) + pwb_ref[...]
    ycat_scr[:, POOL_WIDTH:] = yb.astype(BF16)

    h1 = x + jnp.dot(ycat_scr[...], wout_ref[...], preferred_element_type=F32)
    h1_ref[...] = h1
    ms = jnp.mean(h1 * h1, axis=-1, keepdims=True)
    xn2_ref[...] = (h1 * lax.rsqrt(ms + RMS_EPS) * g2_ref[...]).T.astype(BF16)


def _mixer_call(x, meta, g1, w_in, pool_w, pool_scale, dw_w, dw_b, ln_g, ln_b,
                pw_w, pw_b, w_out, g2, up):
    n_steps = SEQ // MIX_ROWS
    row_spec = pl.BlockSpec((MIX_ROWS, D_MODEL), lambda i: (i, 0))
    table_spec = pl.BlockSpec((PEER_EXPERTS // n_steps, D_MODEL), lambda i: (i, 0))
    return pl.pallas_call(
        _mixer_kernel,
        grid=(n_steps,),
        in_specs=[
            row_spec,
            _const_spec((N_META, D_MODEL)),
            _const_spec((1, D_MODEL)),
            _const_spec((D_MODEL, IN_WIDTH)),
            _const_spec((len(POOL_WINDOWS), POOL_GROUP_DIM, POOL_GROUP_DIM)),
            _const_spec((1, POOL_WIDTH)),
            _const_spec((CONV_KSIZE, CONV_WIDTH)),
            _const_spec((1, CONV_WIDTH)),
            _const_spec((1, CONV_WIDTH)),
            _const_spec((1, CONV_WIDTH)),
            _const_spec((CONV_WIDTH, CONV_WIDTH)),
            _const_spec((1, CONV_WIDTH)),
            _const_spec((D_MODEL, D_MODEL)),
            _const_spec((1, D_MODEL)),
            table_spec,
        ],
        out_specs=[row_spec, pl.BlockSpec((D_MODEL, MIX_ROWS), lambda i: (0, i)), table_spec],
        out_shape=[jax.ShapeDtypeStruct((SEQ, D_MODEL), F32),
                   jax.ShapeDtypeStruct((D_MODEL, SEQ), BF16),
                   jax.ShapeDtypeStruct((PEER_EXPERTS, D_MODEL), BF16)],
        scratch_shapes=[
            pltpu.VMEM((POOL_HALO + MIX_ROWS, POOL_WIDTH), F32),
            pltpu.VMEM((CONV_HALO + MIX_ROWS, CONV_WIDTH), F32),
            pltpu.VMEM((MIX_ROWS, CONV_WIDTH), F32),
            pltpu.VMEM((MIX_ROWS, D_MODEL), BF16),
            pltpu.VMEM((SUBLANES - 1, CONV_CHUNK + CONV_HALO - SUBLANES, CONV_WIDTH), F32),
        ],
        compiler_params=pltpu.CompilerParams(
            dimension_semantics=("arbitrary",), vmem_limit_bytes=VMEM_LIMIT),
        name="mixer",
    )(x, meta, g1, w_in, pool_w, pool_scale, dw_w, dw_b, ln_g, ln_b, pw_w, pw_b, w_out, g2, up)


def _top16_pair(scores, sv_scr):
    cur = scores
    for k in range(PEER_TOPK):
        m = jnp.max(cur, axis=1, keepdims=True)
        cur = jnp.where(cur == m, _RANK_TAGS[k], cur)
        sv_scr[:, k:k + 1, :] = m
    rank = jnp.floor((_RANK_TAGS[0] - cur) * (1.0 / _RANK_TAG_STEP) + 0.5)
    return jnp.where(cur <= _RANK_TAGS[0], rank, float(PEER_TOPK))


def _route_tile(s_pair, sv_scr):
    ranks = _top16_pair(s_pair, sv_scr)
    r1, r2 = ranks[0], ranks[1]
    s1, s2 = s_pair[0], s_pair[1]
    sv1_scr, sv2_scr = sv_scr.at[0], sv_scr.at[1]
    sv2_lo = sv2_scr[0:SUBLANES, :]
    row = lax.broadcasted_iota(jnp.int32, (SUBLANES, LANES), 0)

    cands = [sv1_scr[0:1, :] + sv2_scr[...]]
    for a in range(1, SUBLANES):
        c = sv1_scr[a:a + 1, :] + sv2_lo
        cands.append(jnp.where(row < PEER_TOPK // (a + 1), c, NEG_INF))
    cands.append(sv1_scr[SUBLANES:PEER_TOPK, :] + sv2_scr[0:1, :])

    top = cands[0][0:1, :]
    cur = list(cands)
    thr = top
    for k in range(PEER_TOPK):
        m = cur[0][0:SUBLANES, :]
        m = jnp.maximum(m, cur[0][SUBLANES:, :])
        for c in cur[1:]:
            m = jnp.maximum(m, c)
        thr = jnp.max(m, axis=0, keepdims=True)
        cur = [jnp.where(c == thr, NEG_INF, c) for c in cur]

    zsum = jnp.zeros((1, LANES), F32)
    counts = []
    for a, c in enumerate(cands):
        sel = c >= thr
        e = jnp.where(sel, jnp.exp(c - top), 0.0)
        zsum = zsum + jnp.sum(e, axis=0, keepdims=True)
        if a < SUBLANES:
            counts.append(jnp.sum(sel.astype(F32), axis=0, keepdims=True))
        else:
            tail = sel.astype(F32)
            counts.extend(tail[b:b + 1, :] for b in range(SUBLANES))

    n = jnp.zeros(s1.shape, F32)
    for a in range(PEER_TOPK):
        n = jnp.where(r1 == float(a), counts[a], n)

    inv_z = 1.0 / zsum
    u = jnp.where(r1 < float(PEER_TOPK), jnp.exp(s1 - sv1_scr[0:1, :]), 0.0)
    v = jnp.where(r2 < float(PEER_TOPK), jnp.exp(s2 - sv2_scr[0:1, :]) * inv_z, 0.0)
    return u, v, n, r2


def _pack_bf16(x):
    return pltpu.bitcast(x.astype(BF16), jnp.uint32)


def _route_kernel(xn_ref, wqt_ref, keys_ref, down_ref,
                  u_ref, v_ref, n_ref, r2_ref, downb_ref,
                  qt_scr, s_even, s_odd, sv_scr):
    downb_ref[...] = down_ref[...].astype(BF16)

    qt = jnp.dot(wqt_ref[...], xn_ref[...], preferred_element_type=F32)
    qt_scr[...] = qt.astype(BF16)

    def scores(h, p, s_ref):
        q0 = pl.multiple_of((2 * h + p) * PEER_HALF, PEER_HALF)
        s_ref[p] = jnp.dot(keys_ref[h, p], qt_scr[pl.ds(q0, PEER_HALF), :],
                           preferred_element_type=F32)

    def head(h, s_ref, nxt, nxt_ref):
        for t in range(PEER_TILES):
            if t < 2:
                scores(nxt, t, nxt_ref)
            lanes = slice(t * LANES, (t + 1) * LANES)
            u, v, n, r2 = _route_tile(s_ref[:, :, lanes], sv_scr)
            u_ref[h, t] = u
            v_ref[h, t] = _pack_bf16(v)
            n_ref[h, t] = n
            r2_ref[h, t] = _pack_bf16(r2)

    scores(0, 0, s_even)
    scores(0, 1, s_even)

    def head_pair(g, carry):
        h = 2 * g
        head(h, s_even, h + 1, s_odd)
        head(h + 1, s_odd, jnp.minimum(h + 2, PEER_HEADS - 1), s_even)
        return carry

    lax.fori_loop(0, PEER_HEADS // 2, head_pair, 0)


def _route_call(xn2, wq_t, keys, down):
    n_steps = SEQ // PEER_ROWS
    def fac(rows, dtype):
        shape = jax.ShapeDtypeStruct((PEER_HEADS, SEQ // LANES, rows, LANES), dtype)
        spec = pl.BlockSpec((PEER_HEADS, PEER_TILES, rows, LANES), lambda i: (0, i, 0, 0))
        return shape, spec

    facs = [fac(PEER_NKEYS, F32), fac(PEER_NKEYS // 2, jnp.uint32),
            fac(PEER_NKEYS, F32), fac(PEER_NKEYS // 2, jnp.uint32)]
    table_spec = pl.BlockSpec((PEER_EXPERTS // n_steps, D_MODEL), lambda i: (i, 0))
    return pl.pallas_call(
        _route_kernel,
        grid=(n_steps,),
        in_specs=[
            pl.BlockSpec((D_MODEL, PEER_ROWS), lambda i: (0, i)),
            _const_spec((PEER_QW, D_MODEL)),
            _const_spec((PEER_HEADS, 2, PEER_NKEYS, PEER_HALF)),
            table_spec,
        ],
        out_specs=[spec for _, spec in facs] + [table_spec],
        out_shape=[shape for shape, _ in facs]
        + [jax.ShapeDtypeStruct((PEER_EXPERTS, D_MODEL), BF16)],
        scratch_shapes=[
            pltpu.VMEM((PEER_QW, PEER_ROWS), BF16),
            pltpu.VMEM((2, PEER_NKEYS, PEER_ROWS), F32),
            pltpu.VMEM((2, PEER_NKEYS, PEER_ROWS), F32),
            pltpu.VMEM((2, PEER_TOPK, LANES), F32),
        ],
        compiler_params=pltpu.CompilerParams(
            dimension_semantics=("arbitrary",), vmem_limit_bytes=VMEM_LIMIT),
        name="peer_route",
    )(xn2, wq_t, keys, down)


def _hidden_piece(p, kk, hid_ref, down_ref, xn_ref):
    toks = slice(p * MXU_PIECE, (p + 1) * MXU_PIECE)
    feat = slice(kk * HID_K_PIECE, (kk + 1) * HID_K_PIECE)
    part = jnp.dot(down_ref[:, feat], xn_ref[feat, toks], preferred_element_type=F32)
    if kk == 0:
        hid_ref[:, toks] = part
    else:
        hid_ref[:, toks] += part


def _up_piece(p, m, acc_ref, upt_ref, act_ref):
    toks = slice(p * MXU_PIECE, (p + 1) * MXU_PIECE)
    rows = slice(m * UP_M_PIECE, (m + 1) * UP_M_PIECE)
    acc_ref[rows, toks] += jnp.dot(upt_ref[rows, :], act_ref[:, toks],
                                   preferred_element_type=F32)


def _gate_tile(krow, il, t, hid_ref, act_ref, u_ref, n_ref, v_ref, r2_ref):
    erows = slice(il * PEER_NKEYS, (il + 1) * PEER_NKEYS)
    lanes = slice(t * LANES, (t + 1) * LANES)
    tile = (PEER_NKEYS, LANES)
    w = jnp.zeros(tile, BF16)
    for h in range(PEER_HEADS):
        ui = jnp.broadcast_to(u_ref[h, t, krow:krow + 1, :].astype(BF16), tile)
        ni = jnp.broadcast_to(n_ref[h, t, krow:krow + 1, :].astype(BF16), tile)
        vt = pltpu.bitcast(v_ref[h, t], BF16)
        rt = pltpu.bitcast(r2_ref[h, t], BF16)
        w = w + jnp.where(rt < ni, vt * ui, jnp.zeros((), BF16))
    hid = hid_ref[erows, lanes]
    gelu = 0.5 * hid * (1.0 + lax.erf(hid * (1.0 / math.sqrt(2.0))))
    act_ref[erows, lanes] = w * gelu.astype(BF16)


def _pipeline_part(krow0, hid_cur, act_cur, hid_next, down_ref, xn_ref,
                   act_prev, upt_ref, acc_ref, u_ref, n_ref, v_ref, r2_ref):
    tiles = [(il, t) for il in range(HALF_KEYS) for t in range(PEER_TILES)]
    n_tok = PEER_ROWS // MXU_PIECE
    ups = [functools.partial(_up_piece, p, m, acc_ref, upt_ref, act_prev)
           for p in range(n_tok) for m in range(D_MODEL // UP_M_PIECE)]
    hids = [functools.partial(_hidden_piece, p, kk, hid_next, down_ref, xn_ref)
            for p in range(n_tok) for kk in range(D_MODEL // HID_K_PIECE)]
    for k, (il, t) in enumerate(tiles):
        for pieces in (ups, hids):
            lo = -(-k * len(pieces) // len(tiles))
            hi = -(-(k + 1) * len(pieces) // len(tiles))
            for piece in pieces[lo:hi]:
                piece()
        _gate_tile(krow0 + il, il, t, hid_cur, act_cur, u_ref, n_ref, v_ref, r2_ref)


def _expert_kernel(xn_ref, dn_ref, up_ref, ua_ref, na_ref, va_ref, r2a_ref,
                   ub_ref, nb_ref, vb_ref, r2b_ref, h1_ref, gf_ref, out_ref,
                   hid0, hid1, act0, act1, acc):
    s = pl.program_id(0)
    lo, hi = pl.ds(0, HALF_CHUNK), pl.ds(HALF_CHUNK, HALF_CHUNK)

    @pl.when(s == 0)
    def _():
        hid1[...] = jnp.zeros(hid1.shape, F32)
        act0[...] = jnp.zeros(act0.shape, BF16)
        act1[...] = jnp.zeros(act1.shape, BF16)
        acc[...] = jnp.zeros(acc.shape, F32)

    _pipeline_part(HALF_KEYS, hid1, act1, hid0, dn_ref.at[lo, :], xn_ref,
                   act0, up_ref.at[:, lo], acc, ua_ref, na_ref, va_ref, r2a_ref)

    @pl.when(s == 0)
    def _():
        act0[...] = jnp.zeros(act0.shape, BF16)

    _pipeline_part(0, hid0, act0, hid1, dn_ref.at[hi, :], xn_ref,
                   act1, up_ref.at[:, hi], acc, ub_ref, nb_ref, vb_ref, r2b_ref)

    @pl.when(jnp.logical_and(s % STEPS_PER_BLOCK == 0, s > 0))
    def _():
        h2 = h1_ref[...] + acc[...].T
        ms = jnp.mean(h2 * h2, axis=-1, keepdims=True)
        out_ref[...] = h2 * lax.rsqrt(ms + RMS_EPS) * gf_ref[...]
        acc[...] = jnp.zeros(acc.shape, F32)


def _expert_call(xn2t, down, up_t, u, v, n, r2, h1, gf):
    n_blocks = SEQ // PEER_ROWS
    n_steps = n_blocks * STEPS_PER_BLOCK + 1
    last = n_blocks - 1
    halves = PEER_EXPERTS // HALF_CHUNK

    def blk(f):
        return lambda s: jnp.clip(f(s) // halves, 0, last)

    prev_half = lambda s: 2 * s - 1
    this_half = lambda s: 2 * s

    def key_spec(f):
        return pl.BlockSpec((PEER_HEADS, PEER_TILES, 2 * HALF_KEYS, LANES),
                            lambda s: (0, blk(f)(s), (f(s) % halves) // 2, 0))

    def full_spec(f):
        return pl.BlockSpec((PEER_HEADS, PEER_TILES, PEER_NKEYS // 2, LANES),
                            lambda s: (0, blk(f)(s), 0, 0))

    done_blk = lambda s: (jnp.maximum(s - 1, 0) // STEPS_PER_BLOCK, 0)
    return pl.pallas_call(
        _expert_kernel,
        grid=(n_steps,),
        in_specs=[
            pl.BlockSpec((D_MODEL, PEER_ROWS), lambda s: (0, blk(this_half)(s))),
            pl.BlockSpec((2 * HALF_CHUNK, D_MODEL), lambda s: (s % STEPS_PER_BLOCK, 0)),
            pl.BlockSpec((None, D_MODEL, 2 * HALF_CHUNK),
                         lambda s: ((s + STEPS_PER_BLOCK - 1) % STEPS_PER_BLOCK, 0, 0)),
            key_spec(prev_half), key_spec(prev_half), full_spec(prev_half), full_spec(prev_half),
            key_spec(this_half), key_spec(this_half), full_spec(this_half), full_spec(this_half),
            pl.BlockSpec((PEER_ROWS, D_MODEL), done_blk, pipeline_mode=pl.Buffered(1)),
            pl.BlockSpec((1, D_MODEL), lambda s: (0, 0)),
        ],
        out_specs=pl.BlockSpec((PEER_ROWS, D_MODEL), done_blk),
        out_shape=jax.ShapeDtypeStruct((SEQ, D_MODEL), F32),
        scratch_shapes=[
            pltpu.VMEM((HALF_CHUNK, PEER_ROWS), F32),
            pltpu.VMEM((HALF_CHUNK, PEER_ROWS), F32),
            pltpu.VMEM((HALF_CHUNK, PEER_ROWS), BF16),
            pltpu.VMEM((HALF_CHUNK, PEER_ROWS), BF16),
            pltpu.VMEM((D_MODEL, PEER_ROWS), F32),
        ],
        compiler_params=pltpu.CompilerParams(
            dimension_semantics=("arbitrary",), vmem_limit_bytes=VMEM_LIMIT),
        name="peer_experts",
    )(xn2t, down, up_t, u, n, v, r2, u, n, v, r2, h1, gf)


def kernel(x, meta_tokens, mix_norm_g, w_in, pool_w, pool_scale, conv_dw_w, conv_dw_b, conv_ln_g, conv_ln_b, conv_pw_w, conv_pw_b, w_out, ffn_norm_g, peer_wq, peer_keys, peer_down, peer_up, final_norm_g):
    assert x.shape == (1, SEQ, D_MODEL) and w_in.shape[0] == 1
    row = lambda a: a.reshape(1, -1).astype(F32)
    h1, xn2, up_b = _mixer_call(
        x[0], meta_tokens, row(mix_norm_g[0]), w_in[0].astype(BF16),
        pool_w[0].astype(BF16), row(pool_scale[0]), conv_dw_w[0], row(conv_dw_b[0]),
        row(conv_ln_g[0]), row(conv_ln_b[0]), conv_pw_w[0].astype(BF16),
        row(conv_pw_b[0]), w_out[0].astype(BF16), row(ffn_norm_g[0]), peer_up[0])
    u, v, n, r2, down_b = _route_call(xn2, peer_wq[0].T.astype(BF16),
                                      peer_keys[0].astype(BF16), peer_down[0])
    up_t = up_b.reshape(-1, 2 * HALF_CHUNK, D_MODEL).transpose(0, 2, 1)
    out = _expert_call(xn2, down_b, up_t, u, v, n, r2, h1, row(final_norm_g))
    return out[None]
```

```python
import functools
import math

import jax
import jax.numpy as jnp
from jax import lax
from jax.experimental import pallas as pl
from jax.experimental.pallas import tpu as pltpu

F32 = jnp.float32
BF16 = jnp.bfloat16

D_MODEL = 2048
SEQ = 8192
N_META = 16
POOL_WIDTH = 1024
CONV_WIDTH = 1024
IN_WIDTH = POOL_WIDTH + 2 * CONV_WIDTH
POOL_WINDOWS = (2, 4, 8, 16)
POOL_GROUP_DIM = 256
CONV_KSIZE = 31
PEER_HEADS = 8
PEER_NKEYS = 128
PEER_EXPERTS = PEER_NKEYS * PEER_NKEYS
PEER_HALF = 128
PEER_QW = PEER_HEADS * 2 * PEER_HALF
PEER_TOPK = 16
RMS_EPS = 1e-6
LN_EPS = 1e-5

LANES = 128
SUBLANES = 8
VMEM_LIMIT = 58 * 1024 * 1024

MIX_ROWS = 256
POOL_HALO = 16
CONV_HALO = 32
CONV_CHUNK = 128
PEER_ROWS = 512
PEER_TILES = PEER_ROWS // LANES
HALF_CHUNK = 512
HALF_KEYS = HALF_CHUNK // PEER_NKEYS
STEPS_PER_BLOCK = PEER_EXPERTS // (2 * HALF_CHUNK)
MXU_PIECE = 256
HID_K_PIECE = 512
Q_K_PIECE = 512
UP_M_PIECE = 256
NEG_INF = float("-inf")
_RANK_TAG_STEP = 2.0 ** 104
_RANK_TAGS = [-((2.0 ** 24 - 16 + k) * _RANK_TAG_STEP) for k in range(PEER_TOPK)]


def _const_spec(shape):
    zeros = (0,) * len(shape)
    return pl.BlockSpec(shape, lambda *_: zeros, pipeline_mode=pl.Buffered(1))


def _mixer_kernel(x_ref, meta_ref, g1_ref, win_ref, poolw_ref, pscale_ref,
                  dww_ref, dwb_ref, lng_ref, lnb_ref, pww_ref, pwb_ref,
                  wout_ref, g2_ref, up_ref,
                  h1_ref, xn2_ref, upb_ref,
                  xp_ext, glu_ext, cv_scr, ycat_scr, shift_scr):
    step = pl.program_id(0)
    rows = MIX_ROWS

    upb_ref[...] = up_ref[...].astype(BF16)

    def project(tok):
        ms = jnp.mean(tok * tok, axis=-1, keepdims=True)
        hn = tok * lax.rsqrt(ms + RMS_EPS) * g1_ref[...]
        return jnp.dot(hn.astype(BF16), win_ref[...], preferred_element_type=F32)

    def glu_of(proj):
        xa = proj[:, POOL_WIDTH:POOL_WIDTH + CONV_WIDTH]
        xg = proj[:, POOL_WIDTH + CONV_WIDTH:]
        return xa * jax.nn.sigmoid(xg)

    @pl.when(step == 0)
    def _():
        pm = project(meta_ref[...])
        xp_ext[0:POOL_HALO, :] = pm[:, :POOL_WIDTH]
        glu_ext[0:CONV_HALO - N_META, :] = jnp.zeros((CONV_HALO - N_META, CONV_WIDTH), F32)
        glu_ext[CONV_HALO - N_META:CONV_HALO, :] = glu_of(pm)

    @pl.when(step > 0)
    def _():
        xp_ext[0:POOL_HALO, :] = xp_ext[rows:rows + POOL_HALO, :]
        glu_ext[0:CONV_HALO, :] = glu_ext[rows:rows + CONV_HALO, :]

    x = x_ref[...]
    proj = project(x)
    xp_ext[POOL_HALO:POOL_HALO + rows, :] = proj[:, :POOL_WIDTH]
    glu_ext[CONV_HALO:CONV_HALO + rows, :] = glu_of(proj)

    for g, win in enumerate(POOL_WINDOWS):
        c0 = g * POOL_GROUP_DIM
        cols = slice(c0, c0 + POOL_GROUP_DIM)
        cur = xp_ext[POOL_HALO:POOL_HALO + rows, cols]
        acc = cur
        for k in range(1, win):
            acc = acc + xp_ext[POOL_HALO - k:POOL_HALO - k + rows, cols]
        pooled = acc * (1.0 / win) - cur
        ya = jnp.dot(pooled.astype(BF16), poolw_ref[g], preferred_element_type=F32)
        ycat_scr[:, cols] = (ya * pscale_ref[:, cols]).astype(BF16)

    span = CONV_CHUNK + CONV_HALO - SUBLANES

    def conv_chunk(r, carry):
        r0 = pl.multiple_of(r * CONV_CHUNK, CONV_CHUNK)
        window = glu_ext[pl.ds(r0, CONV_CHUNK + CONV_HALO), :]
        for b in range(1, SUBLANES):
            shift_scr[b - 1] = window[b:b + span, :]
        for lt in range(CONV_WIDTH // LANES):
            lanes = slice(lt * LANES, (lt + 1) * LANES)
            acc = jnp.broadcast_to(dwb_ref[:, lanes], (CONV_CHUNK, LANES))
            for k in range(CONV_KSIZE):
                a, b = divmod(CONV_HALO - (CONV_KSIZE - 1) + k, SUBLANES)
                if b == 0:
                    src = glu_ext[pl.ds(pl.multiple_of(r0 + a * SUBLANES, SUBLANES), CONV_CHUNK), lanes]
                else:
                    src = shift_scr[b - 1, a * SUBLANES:a * SUBLANES + CONV_CHUNK, lanes]
                acc = acc + dww_ref[k:k + 1, lanes] * src
            cv_scr[pl.ds(r0, CONV_CHUNK), lanes] = acc
        return carry

    lax.fori_loop(0, rows // CONV_CHUNK, conv_chunk, 0)

    cv = cv_scr[...]
    mu = jnp.mean(cv, axis=-1, keepdims=True)
    xc = cv - mu
    var = jnp.mean(xc * xc, axis=-1, keepdims=True)
    ln = xc * lax.rsqrt(var + LN_EPS) * lng_ref[...] + lnb_ref[...]
    sw = ln * jax.nn.sigmoid(ln)
    yb = jnp.dot(sw.astype(BF16), pww_ref[...], preferred_element_type=F32) + pwb_ref[...]
    ycat_scr[:, POOL_WIDTH:] = yb.astype(BF16)

    h1 = x + jnp.dot(ycat_scr[...], wout_ref[...], preferred_element_type=F32)
    h1_ref[...] = h1
    ms = jnp.mean(h1 * h1, axis=-1, keepdims=True)
    xn2_ref[...] = (h1 * lax.rsqrt(ms + RMS_EPS) * g2_ref[...]).T.astype(BF16)


def _mixer_call(x, meta, g1, w_in, pool_w, pool_scale, dw_w, dw_b, ln_g, ln_b,
                pw_w, pw_b, w_out, g2, up):
    n_steps = SEQ // MIX_ROWS
    row_spec = pl.BlockSpec((MIX_ROWS, D_MODEL), lambda i: (i, 0))
    table_spec = pl.BlockSpec((PEER_EXPERTS // n_steps, D_MODEL), lambda i: (i, 0))
    return pl.pallas_call(
        _mixer_kernel,
        grid=(n_steps,),
        in_specs=[
            row_spec,
            _const_spec((N_META, D_MODEL)),
            _const_spec((1, D_MODEL)),
            _const_spec((D_MODEL, IN_WIDTH)),
            _const_spec((len(POOL_WINDOWS), POOL_GROUP_DIM, POOL_GROUP_DIM)),
            _const_spec((1, POOL_WIDTH)),
            _const_spec((CONV_KSIZE, CONV_WIDTH)),
            _const_spec((1, CONV_WIDTH)),
            _const_spec((1, CONV_WIDTH)),
            _const_spec((1, CONV_WIDTH)),
            _const_spec((CONV_WIDTH, CONV_WIDTH)),
            _const_spec((1, CONV_WIDTH)),
            _const_spec((D_MODEL, D_MODEL)),
            _const_spec((1, D_MODEL)),
            table_spec,
        ],
        out_specs=[row_spec, pl.BlockSpec((D_MODEL, MIX_ROWS), lambda i: (0, i)), table_spec],
        out_shape=[jax.ShapeDtypeStruct((SEQ, D_MODEL), F32),
                   jax.ShapeDtypeStruct((D_MODEL, SEQ), BF16),
                   jax.ShapeDtypeStruct((PEER_EXPERTS, D_MODEL), BF16)],
        scratch_shapes=[
            pltpu.VMEM((POOL_HALO + MIX_ROWS, POOL_WIDTH), F32),
            pltpu.VMEM((CONV_HALO + MIX_ROWS, CONV_WIDTH), F32),
            pltpu.VMEM((MIX_ROWS, CONV_WIDTH), F32),
            pltpu.VMEM((MIX_ROWS, D_MODEL), BF16),
            pltpu.VMEM((SUBLANES - 1, CONV_CHUNK + CONV_HALO - SUBLANES, CONV_WIDTH), F32),
        ],
        compiler_params=pltpu.CompilerParams(
            dimension_semantics=("arbitrary",), vmem_limit_bytes=VMEM_LIMIT),
        name="mixer",
    )(x, meta, g1, w_in, pool_w, pool_scale, dw_w, dw_b, ln_g, ln_b, pw_w, pw_b, w_out, g2, up)


def _top16_pair(scores, sv_scr):
    cur = scores
    for k in range(PEER_TOPK):
        m = jnp.max(cur, axis=1, keepdims=True)
        cur = jnp.where(cur == m, _RANK_TAGS[k], cur)
        sv_scr[:, k:k + 1, :] = m
    rank = jnp.floor((_RANK_TAGS[0] - cur) * (1.0 / _RANK_TAG_STEP) + 0.5)
    return jnp.where(cur <= _RANK_TAGS[0], rank, float(PEER_TOPK))


def _route_tile(s_pair, sv_scr):
    ranks = _top16_pair(s_pair, sv_scr)
    r1, r2 = ranks[0], ranks[1]
    s1, s2 = s_pair[0], s_pair[1]
    sv1_scr, sv2_scr = sv_scr.at[0], sv_scr.at[1]
    sv2_lo = sv2_scr[0:SUBLANES, :]
    row = lax.broadcasted_iota(jnp.int32, (SUBLANES, LANES), 0)

    cands = [sv1_scr[0:1, :] + sv2_scr[...]]
    for a in range(1, SUBLANES):
        c = sv1_scr[a:a + 1, :] + sv2_lo
        cands.append(jnp.where(row < PEER_TOPK // (a + 1), c, NEG_INF))
    cands.append(sv1_scr[SUBLANES:PEER_TOPK, :] + sv2_scr[0:1, :])

    top = cands[0][0:1, :]
    cur = list(cands)
    thr = top
    for k in range(PEER_TOPK):
        m = cur[0][0:SUBLANES, :]
        m = jnp.maximum(m, cur[0][SUBLANES:, :])
        for c in cur[1:]:
            m = jnp.maximum(m, c)
        thr = jnp.max(m, axis=0, keepdims=True)
        cur = [jnp.where(c == thr, NEG_INF, c) for c in cur]

    zsum = jnp.zeros((1, LANES), F32)
    counts = []
    for a, c in enumerate(cands):
        sel = c >= thr
        e = jnp.where(sel, jnp.exp(c - top), 0.0)
        zsum = zsum + jnp.sum(e, axis=0, keepdims=True)
        if a < SUBLANES:
            counts.append(jnp.sum(sel.astype(F32), axis=0, keepdims=True))
        else:
            tail = sel.astype(F32)
            counts.extend(tail[b:b + 1, :] for b in range(SUBLANES))

    r1_b = r1.astype(BF16)
    n_b = jnp.zeros(s1.shape, BF16)
    for a in range(PEER_TOPK):
        count_b = jnp.broadcast_to(counts[a].astype(BF16), s1.shape)
        n_b = jnp.where(r1_b == jnp.asarray(a, BF16), count_b, n_b)
    n = n_b.astype(F32)

    inv_z = 1.0 / zsum
    u = jnp.where(r1 < float(PEER_TOPK), jnp.exp(s1 - sv1_scr[0:1, :]), 0.0)
    v = jnp.where(r2 < float(PEER_TOPK), jnp.exp(s2 - sv2_scr[0:1, :]) * inv_z, 0.0)
    return u, v, n, r2


def _pack_bf16(x):
    return pltpu.bitcast(x.astype(BF16), jnp.uint32)


def _route_kernel(xn_ref, wqt_ref, keys_ref, down_ref,
                  u_ref, v_ref, n_ref, r2_ref, downb_ref,
                  q_even, q_odd, qacc, s_even, s_odd, sv_scr):
    downb_ref[...] = down_ref[...].astype(BF16)

    q_rows = 2 * PEER_HALF
    n_kk = D_MODEL // Q_K_PIECE
    assert n_kk == PEER_TILES

    def q_piece(h, kk, q_ref):
        r0 = pl.multiple_of(h * q_rows, q_rows)
        feat = slice(kk * Q_K_PIECE, (kk + 1) * Q_K_PIECE)
        part = jnp.dot(wqt_ref[pl.ds(r0, q_rows), feat], xn_ref[feat, :],
                       preferred_element_type=F32)
        if kk == 0:
            qacc[...] = part
        elif kk < n_kk - 1:
            qacc[...] += part
        else:
            q_ref[...] = (qacc[...] + part).astype(BF16)

    def scores(h, p, q_ref, s_ref):
        s_ref[p] = jnp.dot(keys_ref[h, p], q_ref[p * PEER_HALF:(p + 1) * PEER_HALF, :],
                           preferred_element_type=F32)

    def head(h, s_ref, q_new_ref, q_nxt_ref, s_nxt_ref):
        last = PEER_HEADS - 1
        for t in range(PEER_TILES):
            q_piece(jnp.minimum(h + 2, last), t, q_new_ref)
            if t < 2:
                scores(jnp.minimum(h + 1, last), t, q_nxt_ref, s_nxt_ref)
            lanes = slice(t * LANES, (t + 1) * LANES)
            u, v, n, r2 = _route_tile(s_ref[:, :, lanes], sv_scr)
            u_ref[h, t] = u
            v_ref[h, t] = _pack_bf16(v)
            n_ref[h, t] = n
            r2_ref[h, t] = _pack_bf16(r2)

    for kk in range(n_kk):
        q_piece(0, kk, q_even)
    for kk in range(n_kk):
        q_piece(1, kk, q_odd)
    scores(0, 0, q_even, s_even)
    scores(0, 1, q_even, s_even)

    def head_pair(g, carry):
        h = 2 * g
        head(h, s_even, q_even, q_odd, s_odd)
        head(h + 1, s_odd, q_odd, q_even, s_even)
        return carry

    lax.fori_loop(0, PEER_HEADS // 2, head_pair, 0)


def _route_call(xn2, wq_t, keys, down):
    n_steps = SEQ // PEER_ROWS
    def fac(rows, dtype):
        shape = jax.ShapeDtypeStruct((PEER_HEADS, SEQ // LANES, rows, LANES), dtype)
        spec = pl.BlockSpec((PEER_HEADS, PEER_TILES, rows, LANES), lambda i: (0, i, 0, 0))
        return shape, spec

    facs = [fac(PEER_NKEYS, F32), fac(PEER_NKEYS // 2, jnp.uint32),
            fac(PEER_NKEYS, F32), fac(PEER_NKEYS // 2, jnp.uint32)]
    table_spec = pl.BlockSpec((PEER_EXPERTS // n_steps, D_MODEL), lambda i: (i, 0))
    return pl.pallas_call(
        _route_kernel,
        grid=(n_steps,),
        in_specs=[
            pl.BlockSpec((D_MODEL, PEER_ROWS), lambda i: (0, i)),
            _const_spec((PEER_QW, D_MODEL)),
            _const_spec((PEER_HEADS, 2, PEER_NKEYS, PEER_HALF)),
            table_spec,
        ],
        out_specs=[spec for _, spec in facs] + [table_spec],
        out_shape=[shape for shape, _ in facs]
        + [jax.ShapeDtypeStruct((PEER_EXPERTS, D_MODEL), BF16)],
        scratch_shapes=[
            pltpu.VMEM((2 * PEER_HALF, PEER_ROWS), BF16),
            pltpu.VMEM((2 * PEER_HALF, PEER_ROWS), BF16),
            pltpu.VMEM((2 * PEER_HALF, PEER_ROWS), F32),
            pltpu.VMEM((2, PEER_NKEYS, PEER_ROWS), F32),
            pltpu.VMEM((2, PEER_NKEYS, PEER_ROWS), F32),
            pltpu.VMEM((2, PEER_TOPK, LANES), F32),
        ],
        compiler_params=pltpu.CompilerParams(
            dimension_semantics=("arbitrary",), vmem_limit_bytes=VMEM_LIMIT),
        name="peer_route",
    )(xn2, wq_t, keys, down)


def _hidden_piece(p, kk, hid_ref, down_ref, xn_ref):
    toks = slice(p * MXU_PIECE, (p + 1) * MXU_PIECE)
    feat = slice(kk * HID_K_PIECE, (kk + 1) * HID_K_PIECE)
    part = jnp.dot(down_ref[:, feat], xn_ref[feat, toks], preferred_element_type=F32)
    if kk == 0:
        hid_ref[:, toks] = part
    else:
        hid_ref[:, toks] += part


def _up_piece(p, m, acc_ref, upt_ref, act_ref):
    toks = slice(p * MXU_PIECE, (p + 1) * MXU_PIECE)
    rows = slice(m * UP_M_PIECE, (m + 1) * UP_M_PIECE)
    acc_ref[rows, toks] += jnp.dot(upt_ref[rows, :], act_ref[:, toks],
                                   preferred_element_type=F32)


def _gate_tile(krow, il, t, hid_ref, act_ref, u_ref, n_ref, v_ref, r2_ref):
    erows = slice(il * PEER_NKEYS, (il + 1) * PEER_NKEYS)
    lanes = slice(t * LANES, (t + 1) * LANES)
    tile = (PEER_NKEYS, LANES)
    w = jnp.zeros(tile, BF16)
    for h in range(PEER_HEADS):
        ui = jnp.broadcast_to(u_ref[h, t, krow:krow + 1, :].astype(BF16), tile)
        ni = jnp.broadcast_to(n_ref[h, t, krow:krow + 1, :].astype(BF16), tile)
        vt = pltpu.bitcast(v_ref[h, t], BF16)
        rt = pltpu.bitcast(r2_ref[h, t], BF16)
        w = w + jnp.where(rt < ni, vt * ui, jnp.zeros((), BF16))
    hid = hid_ref[erows, lanes]
    gelu = 0.5 * hid * (1.0 + lax.erf(hid * (1.0 / math.sqrt(2.0))))
    act_ref[erows, lanes] = w * gelu.astype(BF16)


def _pipeline_part(krow0, hid_cur, act_cur, hid_next, down_ref, xn_ref,
                   act_prev, upt_ref, acc_ref, u_ref, n_ref, v_ref, r2_ref):
    tiles = [(il, t) for il in range(HALF_KEYS) for t in range(PEER_TILES)]
    n_tok = PEER_ROWS // MXU_PIECE
    ups = [functools.partial(_up_piece, p, m, acc_ref, upt_ref, act_prev)
           for p in range(n_tok) for m in range(D_MODEL // UP_M_PIECE)]
    hids = [functools.partial(_hidden_piece, p, kk, hid_next, down_ref, xn_ref)
            for p in range(n_tok) for kk in range(D_MODEL // HID_K_PIECE)]
    for k, (il, t) in enumerate(tiles):
        for pieces in (ups, hids):
            lo = -(-k * len(pieces) // len(tiles))
            hi = -(-(k + 1) * len(pieces) // len(tiles))
            for piece in pieces[lo:hi]:
                piece()
        _gate_tile(krow0 + il, il, t, hid_cur, act_cur, u_ref, n_ref, v_ref, r2_ref)


def _expert_kernel(xn_ref, dn_ref, up_ref, ua_ref, na_ref, va_ref, r2a_ref,
                   ub_ref, nb_ref, vb_ref, r2b_ref, h1_ref, gf_ref, out_ref,
                   hid0, hid1, act0, act1, acc):
    s = pl.program_id(0)
    lo, hi = pl.ds(0, HALF_CHUNK), pl.ds(HALF_CHUNK, HALF_CHUNK)

    @pl.when(s == 0)
    def _():
        hid1[...] = jnp.zeros(hid1.shape, F32)
        act0[...] = jnp.zeros(act0.shape, BF16)
        act1[...] = jnp.zeros(act1.shape, BF16)
        acc[...] = jnp.zeros(acc.shape, F32)

    _pipeline_part(HALF_KEYS, hid1, act1, hid0, dn_ref.at[lo, :], xn_ref,
                   act0, up_ref.at[:, lo], acc, ua_ref, na_ref, va_ref, r2a_ref)

    @pl.when(s == 0)
    def _():
        act0[...] = jnp.zeros(act0.shape, BF16)

    _pipeline_part(0, hid0, act0, hid1, dn_ref.at[hi, :], xn_ref,
                   act1, up_ref.at[:, hi], acc, ub_ref, nb_ref, vb_ref, r2b_ref)

    @pl.when(jnp.logical_and(s % STEPS_PER_BLOCK == 0, s > 0))
    def _():
        h2 = h1_ref[...] + acc[...].T
        ms = jnp.mean(h2 * h2, axis=-1, keepdims=True)
        out_ref[...] = h2 * lax.rsqrt(ms + RMS_EPS) * gf_ref[...]
        acc[...] = jnp.zeros(acc.shape, F32)


def _expert_call(xn2t, down, up_t, u, v, n, r2, h1, gf):
    n_blocks = SEQ // PEER_ROWS
    n_steps = n_blocks * STEPS_PER_BLOCK + 1
    last = n_blocks - 1
    halves = PEER_EXPERTS // HALF_CHUNK

    def blk(f):
        return lambda s: jnp.clip(f(s) // halves, 0, last)

    prev_half = lambda s: 2 * s - 1
    this_half = lambda s: 2 * s

    def key_spec(f):
        return pl.BlockSpec((PEER_HEADS, PEER_TILES, 2 * HALF_KEYS, LANES),
                            lambda s: (0, blk(f)(s), (f(s) % halves) // 2, 0))

    def full_spec(f):
        return pl.BlockSpec((PEER_HEADS, PEER_TILES, PEER_NKEYS // 2, LANES),
                            lambda s: (0, blk(f)(s), 0, 0))

    done_blk = lambda s: (jnp.maximum(s - 1, 0) // STEPS_PER_BLOCK, 0)
    return pl.pallas_call(
        _expert_kernel,
        grid=(n_steps,),
        in_specs=[
            pl.BlockSpec((D_MODEL, PEER_ROWS), lambda s: (0, blk(this_half)(s))),
            pl.BlockSpec((2 * HALF_CHUNK, D_MODEL), lambda s: (s % STEPS_PER_BLOCK, 0)),
            pl.BlockSpec((None, D_MODEL, 2 * HALF_CHUNK),
                         lambda s: ((s + STEPS_PER_BLOCK - 1) % STEPS_PER_BLOCK, 0, 0)),
            key_spec(prev_half), key_spec(prev_half), full_spec(prev_half), full_spec(prev_half),
            key_spec(this_half), key_spec(this_half), full_spec(this_half), full_spec(this_half),
            pl.BlockSpec((PEER_ROWS, D_MODEL), done_blk, pipeline_mode=pl.Buffered(1)),
            pl.BlockSpec((1, D_MODEL), lambda s: (0, 0)),
        ],
        out_specs=pl.BlockSpec((PEER_ROWS, D_MODEL), done_blk),
        out_shape=jax.ShapeDtypeStruct((SEQ, D_MODEL), F32),
        scratch_shapes=[
            pltpu.VMEM((HALF_CHUNK, PEER_ROWS), F32),
            pltpu.VMEM((HALF_CHUNK, PEER_ROWS), F32),
            pltpu.VMEM((HALF_CHUNK, PEER_ROWS), BF16),
            pltpu.VMEM((HALF_CHUNK, PEER_ROWS), BF16),
            pltpu.VMEM((D_MODEL, PEER_ROWS), F32),
        ],
        compiler_params=pltpu.CompilerParams(
            dimension_semantics=("arbitrary",), vmem_limit_bytes=VMEM_LIMIT),
        name="peer_experts",
    )(xn2t, down, up_t, u, n, v, r2, u, n, v, r2, h1, gf)


def kernel(x, meta_tokens, mix_norm_g, w_in, pool_w, pool_scale, conv_dw_w, conv_dw_b, conv_ln_g, conv_ln_b, conv_pw_w, conv_pw_b, w_out, ffn_norm_g, peer_wq, peer_keys, peer_down, peer_up, final_norm_g):
    assert x.shape == (1, SEQ, D_MODEL) and w_in.shape[0] == 1
    row = lambda a: a.reshape(1, -1).astype(F32)
    h1, xn2, up_b = _mixer_call(
        x[0], meta_tokens, row(mix_norm_g[0]), w_in[0].astype(BF16),
        pool_w[0].astype(BF16), row(pool_scale[0]), conv_dw_w[0], row(conv_dw_b[0]),
        row(conv_ln_g[0]), row(conv_ln_b[0]), conv_pw_w[0].astype(BF16),
        row(conv_pw_b[0]), w_out[0].astype(BF16), row(ffn_norm_g[0]), peer_up[0])
    u, v, n, r2, down_b = _route_call(xn2, peer_wq[0].T.astype(BF16),
                                      peer_keys[0].astype(BF16), peer_down[0])
    up_t = up_b.reshape(-1, 2 * HALF_CHUNK, D_MODEL).transpose(0, 2, 1)
    out = _expert_call(xn2, down_b, up_t, u, v, n, r2, h1, row(final_norm_g))
    return out[None]
```

```python
import functools
import math

import jax
import jax.numpy as jnp
from jax import lax
from jax.experimental import pallas as pl
from jax.experimental.pallas import tpu as pltpu

F32 = jnp.float32
BF16 = jnp.bfloat16

D_MODEL = 2048
SEQ = 8192
N_META = 16
POOL_WIDTH = 1024
CONV_WIDTH = 1024
IN_WIDTH = POOL_WIDTH + 2 * CONV_WIDTH
POOL_WINDOWS = (2, 4, 8, 16)
POOL_GROUP_DIM = 256
CONV_KSIZE = 31
PEER_HEADS = 8
PEER_NKEYS = 128
PEER_EXPERTS = PEER_NKEYS * PEER_NKEYS
PEER_HALF = 128
PEER_QW = PEER_HEADS * 2 * PEER_HALF
PEER_TOPK = 16
RMS_EPS = 1e-6
LN_EPS = 1e-5

LANES = 128
SUBLANES = 8
VMEM_LIMIT = 58 * 1024 * 1024

MIX_ROWS = 256
POOL_HALO = 16
CONV_HALO = 32
CONV_CHUNK = 128
PEER_ROWS = 512
PEER_TILES = PEER_ROWS // LANES
HALF_CHUNK = 512
HALF_KEYS = HALF_CHUNK // PEER_NKEYS
STEPS_PER_BLOCK = PEER_EXPERTS // (2 * HALF_CHUNK)
MXU_PIECE = 256
HID_K_PIECE = 512
Q_K_PIECE = 512
UP_M_PIECE = 256
NEG_INF = float("-inf")
_RANK_TAG_STEP = 2.0 ** 104
_RANK_TAGS = [-((2.0 ** 24 - 16 + k) * _RANK_TAG_STEP) for k in range(PEER_TOPK)]


def _const_spec(shape):
    zeros = (0,) * len(shape)
    return pl.BlockSpec(shape, lambda *_: zeros, pipeline_mode=pl.Buffered(1))


def _mixer_kernel(x_ref, meta_ref, g1_ref, win_ref, poolw_ref, pscale_ref,
                  dww_ref, dwb_ref, lng_ref, lnb_ref, pww_ref, pwb_ref,
                  wout_ref, g2_ref, up_ref,
                  h1_ref, xn2_ref, upb_ref,
                  xp_ext, glu_ext, cv_scr, ycat_scr, shift_scr, mixa_scr):
    step = pl.program_id(0)
    rows = MIX_ROWS

    upb_ref[...] = up_ref[...].astype(BF16)

    def project(tok):
        ms = jnp.mean(tok * tok, axis=-1, keepdims=True)
        hn = tok * lax.rsqrt(ms + RMS_EPS) * g1_ref[...]
        return jnp.dot(hn.astype(BF16), win_ref[...], preferred_element_type=F32)

    def glu_of(proj):
        xa = proj[:, POOL_WIDTH:POOL_WIDTH + CONV_WIDTH]
        xg = proj[:, POOL_WIDTH + CONV_WIDTH:]
        return xa * jax.nn.sigmoid(xg)

    @pl.when(step == 0)
    def _():
        pm = project(meta_ref[...])
        xp_ext[0:POOL_HALO, :] = pm[:, :POOL_WIDTH]
        glu_ext[0:CONV_HALO - N_META, :] = jnp.zeros((CONV_HALO - N_META, CONV_WIDTH), F32)
        glu_ext[CONV_HALO - N_META:CONV_HALO, :] = glu_of(pm)

    @pl.when(step > 0)
    def _():
        xp_ext[0:POOL_HALO, :] = xp_ext[rows:rows + POOL_HALO, :]
        glu_ext[0:CONV_HALO, :] = glu_ext[rows:rows + CONV_HALO, :]

    x = x_ref[...]
    proj = project(x)
    xp_ext[POOL_HALO:POOL_HALO + rows, :] = proj[:, :POOL_WIDTH]
    glu_ext[CONV_HALO:CONV_HALO + rows, :] = glu_of(proj)

    for g, win in enumerate(POOL_WINDOWS):
        c0 = g * POOL_GROUP_DIM
        cols = slice(c0, c0 + POOL_GROUP_DIM)
        cur = xp_ext[POOL_HALO:POOL_HALO + rows, cols]
        acc = cur
        for k in range(1, win):
            acc = acc + xp_ext[POOL_HALO - k:POOL_HALO - k + rows, cols]
        pooled = acc * (1.0 / win) - cur
        ya = jnp.dot(pooled.astype(BF16), poolw_ref[g], preferred_element_type=F32)
        ycat_scr[:, cols] = (ya * pscale_ref[:, cols]).astype(BF16)

    span = CONV_CHUNK + CONV_HALO - SUBLANES

    def out_a_piece(j):
        cols = slice(j * MXU_PIECE, (j + 1) * MXU_PIECE)
        mixa_scr[:, cols] = jnp.dot(ycat_scr[:, :POOL_WIDTH], wout_ref[:POOL_WIDTH, cols],
                                    preferred_element_type=F32)

    lane_tiles = CONV_WIDTH // LANES
    n_conv = (rows // CONV_CHUNK) * lane_tiles
    n_out_a = D_MODEL // MXU_PIECE
    for r in range(rows // CONV_CHUNK):
        r0 = r * CONV_CHUNK
        window = glu_ext[r0:r0 + CONV_CHUNK + CONV_HALO, :]
        for b in range(1, SUBLANES):
            shift_scr[b - 1] = window[b:b + span, :]
        for lt in range(lane_tiles):
            piece = r * lane_tiles + lt
            for j in range(-(-piece * n_out_a // n_conv), -(-(piece + 1) * n_out_a // n_conv)):
                out_a_piece(j)
            lanes = slice(lt * LANES, (lt + 1) * LANES)
            acc = jnp.broadcast_to(dwb_ref[:, lanes], (CONV_CHUNK, LANES))
            for k in range(CONV_KSIZE):
                a, b = divmod(CONV_HALO - (CONV_KSIZE - 1) + k, SUBLANES)
                if b == 0:
                    src = glu_ext[r0 + a * SUBLANES:r0 + a * SUBLANES + CONV_CHUNK, lanes]
                else:
                    src = shift_scr[b - 1, a * SUBLANES:a * SUBLANES + CONV_CHUNK, lanes]
                acc = acc + dww_ref[k:k + 1, lanes] * src
            cv_scr[r0:r0 + CONV_CHUNK, lanes] = acc

    cv = cv_scr[...]
    mu = jnp.mean(cv, axis=-1, keepdims=True)
    xc = cv - mu
    var = jnp.mean(xc * xc, axis=-1, keepdims=True)
    ln = xc * lax.rsqrt(var + LN_EPS) * lng_ref[...] + lnb_ref[...]
    sw = ln * jax.nn.sigmoid(ln)
    yb = jnp.dot(sw.astype(BF16), pww_ref[...], preferred_element_type=F32) + pwb_ref[...]
    ycat_scr[:, POOL_WIDTH:] = yb.astype(BF16)

    h1 = x + (mixa_scr[...] + jnp.dot(ycat_scr[:, POOL_WIDTH:], wout_ref[POOL_WIDTH:, :],
                                      preferred_element_type=F32))
    h1_ref[...] = h1
    ms = jnp.mean(h1 * h1, axis=-1, keepdims=True)
    xn2_ref[...] = (h1 * lax.rsqrt(ms + RMS_EPS) * g2_ref[...]).T.astype(BF16)


def _mixer_call(x, meta, g1, w_in, pool_w, pool_scale, dw_w, dw_b, ln_g, ln_b,
                pw_w, pw_b, w_out, g2, up):
    n_steps = SEQ // MIX_ROWS
    row_spec = pl.BlockSpec((MIX_ROWS, D_MODEL), lambda i: (i, 0))
    table_spec = pl.BlockSpec((PEER_EXPERTS // n_steps, D_MODEL), lambda i: (i, 0))
    return pl.pallas_call(
        _mixer_kernel,
        grid=(n_steps,),
        in_specs=[
            row_spec,
            _const_spec((N_META, D_MODEL)),
            _const_spec((1, D_MODEL)),
            _const_spec((D_MODEL, IN_WIDTH)),
            _const_spec((len(POOL_WINDOWS), POOL_GROUP_DIM, POOL_GROUP_DIM)),
            _const_spec((1, POOL_WIDTH)),
            _const_spec((CONV_KSIZE, CONV_WIDTH)),
            _const_spec((1, CONV_WIDTH)),
            _const_spec((1, CONV_WIDTH)),
            _const_spec((1, CONV_WIDTH)),
            _const_spec((CONV_WIDTH, CONV_WIDTH)),
            _const_spec((1, CONV_WIDTH)),
            _const_spec((D_MODEL, D_MODEL)),
            _const_spec((1, D_MODEL)),
            table_spec,
        ],
        out_specs=[row_spec, pl.BlockSpec((D_MODEL, MIX_ROWS), lambda i: (0, i)), table_spec],
        out_shape=[jax.ShapeDtypeStruct((SEQ, D_MODEL), F32),
                   jax.ShapeDtypeStruct((D_MODEL, SEQ), BF16),
                   jax.ShapeDtypeStruct((PEER_EXPERTS, D_MODEL), BF16)],
        scratch_shapes=[
            pltpu.VMEM((POOL_HALO + MIX_ROWS, POOL_WIDTH), F32),
            pltpu.VMEM((CONV_HALO + MIX_ROWS, CONV_WIDTH), F32),
            pltpu.VMEM((MIX_ROWS, CONV_WIDTH), F32),
            pltpu.VMEM((MIX_ROWS, D_MODEL), BF16),
            pltpu.VMEM((SUBLANES - 1, CONV_CHUNK + CONV_HALO - SUBLANES, CONV_WIDTH), F32),
            pltpu.VMEM((MIX_ROWS, D_MODEL), F32),
        ],
        compiler_params=pltpu.CompilerParams(
            dimension_semantics=("arbitrary",), vmem_limit_bytes=VMEM_LIMIT),
        name="mixer",
    )(x, meta, g1, w_in, pool_w, pool_scale, dw_w, dw_b, ln_g, ln_b, pw_w, pw_b, w_out, g2, up)


def _top16_pair(scores, sv_scr):
    cur = scores
    for k in range(PEER_TOPK):
        m = jnp.max(cur, axis=1, keepdims=True)
        cur = jnp.where(cur == m, _RANK_TAGS[k], cur)
        sv_scr[:, k:k + 1, :] = m
    rank = jnp.floor((_RANK_TAGS[0] - cur) * (1.0 / _RANK_TAG_STEP) + 0.5)
    return jnp.where(cur <= _RANK_TAGS[0], rank, float(PEER_TOPK))


def _route_tile(s_pair, sv_scr):
    ranks = _top16_pair(s_pair, sv_scr)
    r1, r2 = ranks[0], ranks[1]
    s1, s2 = s_pair[0], s_pair[1]
    sv1_scr, sv2_scr = sv_scr.at[0], sv_scr.at[1]
    sv2_lo = sv2_scr[0:SUBLANES, :]
    row = lax.broadcasted_iota(jnp.int32, (SUBLANES, LANES), 0)

    cands = [sv1_scr[0:1, :] + sv2_scr[...]]
    for a in range(1, SUBLANES):
        c = sv1_scr[a:a + 1, :] + sv2_lo
        cands.append(jnp.where(row < PEER_TOPK // (a + 1), c, NEG_INF))
    cands.append(sv1_scr[SUBLANES:PEER_TOPK, :] + sv2_scr[0:1, :])

    top = cands[0][0:1, :]
    cur = list(cands)
    thr = top
    for k in range(PEER_TOPK):
        m = cur[0][0:SUBLANES, :]
        m = jnp.maximum(m, cur[0][SUBLANES:, :])
        for c in cur[1:]:
            m = jnp.maximum(m, c)
        thr = jnp.max(m, axis=0, keepdims=True)
        cur = [jnp.where(c == thr, NEG_INF, c) for c in cur]

    zsum = jnp.zeros((1, LANES), F32)
    counts = []
    for a, c in enumerate(cands):
        sel = c >= thr
        e = jnp.where(sel, jnp.exp(c - top), 0.0)
        zsum = zsum + jnp.sum(e, axis=0, keepdims=True)
        if a < SUBLANES:
            counts.append(jnp.sum(sel.astype(F32), axis=0, keepdims=True))
        else:
            tail = sel.astype(F32)
            counts.extend(tail[b:b + 1, :] for b in range(SUBLANES))

    r1_b = r1.astype(BF16)
    n_b = jnp.zeros(s1.shape, BF16)
    for a in range(PEER_TOPK):
        count_b = jnp.broadcast_to(counts[a].astype(BF16), s1.shape)
        n_b = jnp.where(r1_b == jnp.asarray(a, BF16), count_b, n_b)
    n = n_b.astype(F32)

    inv_z = 1.0 / zsum
    u = jnp.where(r1 < float(PEER_TOPK), jnp.exp(s1 - sv1_scr[0:1, :]), 0.0)
    v = jnp.where(r2 < float(PEER_TOPK), jnp.exp(s2 - sv2_scr[0:1, :]) * inv_z, 0.0)
    return u, v, n, r2


def _pack_bf16(x):
    return pltpu.bitcast(x.astype(BF16), jnp.uint32)


def _route_kernel(xn_ref, wqt_ref, keys_ref, down_ref,
                  un_ref, vr_ref, downb_ref,
                  q_even, q_odd, qacc, s_even, s_odd, sv_scr):
    downb_ref[...] = down_ref[...].astype(BF16)

    q_rows = 2 * PEER_HALF
    n_kk = D_MODEL // Q_K_PIECE
    assert n_kk == PEER_TILES

    def q_piece(h, kk, q_ref):
        r0 = pl.multiple_of(h * q_rows, q_rows)
        feat = slice(kk * Q_K_PIECE, (kk + 1) * Q_K_PIECE)
        part = jnp.dot(wqt_ref[pl.ds(r0, q_rows), feat], xn_ref[feat, :],
                       preferred_element_type=F32)
        if kk == 0:
            qacc[...] = part
        elif kk < n_kk - 1:
            qacc[...] += part
        else:
            q_ref[...] = (qacc[...] + part).astype(BF16)

    def scores(h, p, q_ref, s_ref):
        s_ref[p] = jnp.dot(keys_ref[h, p], q_ref[p * PEER_HALF:(p + 1) * PEER_HALF, :],
                           preferred_element_type=F32)

    def head(h, s_ref, q_new_ref, q_nxt_ref, s_nxt_ref):
        last = PEER_HEADS - 1
        for t in range(PEER_TILES):
            q_piece(jnp.minimum(h + 2, last), t, q_new_ref)
            if t < 2:
                scores(jnp.minimum(h + 1, last), t, q_nxt_ref, s_nxt_ref)
            lanes = slice(t * LANES, (t + 1) * LANES)
            u, v, n, r2 = _route_tile(s_ref[:, :, lanes], sv_scr)
            un_ref[0, h, t] = u
            un_ref[1, h, t] = n
            vr_ref[0, h, t] = _pack_bf16(v)
            vr_ref[1, h, t] = _pack_bf16(r2)

    for kk in range(n_kk):
        q_piece(0, kk, q_even)
    for kk in range(n_kk):
        q_piece(1, kk, q_odd)
    scores(0, 0, q_even, s_even)
    scores(0, 1, q_even, s_even)

    def head_pair(g, carry):
        h = 2 * g
        head(h, s_even, q_even, q_odd, s_odd)
        head(h + 1, s_odd, q_odd, q_even, s_even)
        return carry

    lax.fori_loop(0, PEER_HEADS // 2, head_pair, 0)


def _route_call(xn2, wq_t, keys, down):
    n_steps = SEQ // PEER_ROWS
    def fac(rows, dtype):
        shape = jax.ShapeDtypeStruct((2, PEER_HEADS, SEQ // LANES, rows, LANES), dtype)
        spec = pl.BlockSpec((2, PEER_HEADS, PEER_TILES, rows, LANES),
                            lambda i: (0, 0, i, 0, 0))
        return shape, spec

    facs = [fac(PEER_NKEYS, F32), fac(PEER_NKEYS // 2, jnp.uint32)]
    table_spec = pl.BlockSpec((PEER_EXPERTS // n_steps, D_MODEL), lambda i: (i, 0))
    return pl.pallas_call(
        _route_kernel,
        grid=(n_steps,),
        in_specs=[
            pl.BlockSpec((D_MODEL, PEER_ROWS), lambda i: (0, i)),
            _const_spec((PEER_QW, D_MODEL)),
            _const_spec((PEER_HEADS, 2, PEER_NKEYS, PEER_HALF)),
            table_spec,
        ],
        out_specs=[spec for _, spec in facs] + [table_spec],
        out_shape=[shape for shape, _ in facs]
        + [jax.ShapeDtypeStruct((PEER_EXPERTS, D_MODEL), BF16)],
        scratch_shapes=[
            pltpu.VMEM((2 * PEER_HALF, PEER_ROWS), BF16),
            pltpu.VMEM((2 * PEER_HALF, PEER_ROWS), BF16),
            pltpu.VMEM((2 * PEER_HALF, PEER_ROWS), F32),
            pltpu.VMEM((2, PEER_NKEYS, PEER_ROWS), F32),
            pltpu.VMEM((2, PEER_NKEYS, PEER_ROWS), F32),
            pltpu.VMEM((2, PEER_TOPK, LANES), F32),
        ],
        compiler_params=pltpu.CompilerParams(
            dimension_semantics=("arbitrary",), vmem_limit_bytes=VMEM_LIMIT),
        name="peer_route",
    )(xn2, wq_t, keys, down)


def _hidden_piece(p, kk, hid_ref, down_ref, xn_ref):
    toks = slice(p * MXU_PIECE, (p + 1) * MXU_PIECE)
    feat = slice(kk * HID_K_PIECE, (kk + 1) * HID_K_PIECE)
    part = jnp.dot(down_ref[:, feat], xn_ref[feat, toks], preferred_element_type=F32)
    if kk == 0:
        hid_ref[:, toks] = part
    else:
        hid_ref[:, toks] += part


def _up_piece(p, m, acc_ref, upt_ref, act_ref):
    toks = slice(p * MXU_PIECE, (p + 1) * MXU_PIECE)
    rows = slice(m * UP_M_PIECE, (m + 1) * UP_M_PIECE)
    acc_ref[rows, toks] += jnp.dot(upt_ref[rows, :], act_ref[:, toks],
                                   preferred_element_type=F32)


def _gate_tile(krow, il, t, hid_ref, act_ref, u_ref, n_ref, v_ref, r2_ref):
    erows = slice(il * PEER_NKEYS, (il + 1) * PEER_NKEYS)
    lanes = slice(t * LANES, (t + 1) * LANES)
    tile = (PEER_NKEYS, LANES)
    w = jnp.zeros(tile, BF16)
    for h in range(PEER_HEADS):
        ui = jnp.broadcast_to(u_ref[h, t, krow:krow + 1, :].astype(BF16), tile)
        ni = jnp.broadcast_to(n_ref[h, t, krow:krow + 1, :].astype(BF16), tile)
        vt = pltpu.bitcast(v_ref[h, t], BF16)
        rt = pltpu.bitcast(r2_ref[h, t], BF16)
        w = w + jnp.where(rt < ni, vt * ui, jnp.zeros((), BF16))
    hid = hid_ref[erows, lanes]
    gelu = 0.5 * hid * (1.0 + lax.erf(hid * (1.0 / math.sqrt(2.0))))
    act_ref[erows, lanes] = w * gelu.astype(BF16)


def _pipeline_part(krow0, hid_cur, act_cur, hid_next, down_ref, xn_ref,
                   act_prev, upt_ref, acc_ref, u_ref, n_ref, v_ref, r2_ref):
    tiles = [(il, t) for il in range(HALF_KEYS) for t in range(PEER_TILES)]
    n_tok = PEER_ROWS // MXU_PIECE
    ups = [functools.partial(_up_piece, p, m, acc_ref, upt_ref, act_prev)
           for p in range(n_tok) for m in range(D_MODEL // UP_M_PIECE)]
    hids = [functools.partial(_hidden_piece, p, kk, hid_next, down_ref, xn_ref)
            for p in range(n_tok) for kk in range(D_MODEL // HID_K_PIECE)]
    for k, (il, t) in enumerate(tiles):
        for pieces in (ups, hids):
            lo = -(-k * len(pieces) // len(tiles))
            hi = -(-(k + 1) * len(pieces) // len(tiles))
            for piece in pieces[lo:hi]:
                piece()
        _gate_tile(krow0 + il, il, t, hid_cur, act_cur, u_ref, n_ref, v_ref, r2_ref)


def _expert_kernel(xn_ref, dn_ref, up_ref, una_ref, vra_ref, unb_ref, vrb_ref,
                   h1_ref, gf_ref, out_ref, hid0, hid1, act0, act1, acc):
    s = pl.program_id(0)
    lo, hi = pl.ds(0, HALF_CHUNK), pl.ds(HALF_CHUNK, HALF_CHUNK)

    @pl.when(s == 0)
    def _():
        hid1[...] = jnp.zeros(hid1.shape, F32)
        act0[...] = jnp.zeros(act0.shape, BF16)
        act1[...] = jnp.zeros(act1.shape, BF16)
        acc[...] = jnp.zeros(acc.shape, F32)

    _pipeline_part(HALF_KEYS, hid1, act1, hid0, dn_ref.at[lo, :], xn_ref,
                   act0, up_ref.at[:, lo], acc,
                   una_ref.at[0], una_ref.at[1], vra_ref.at[0], vra_ref.at[1])

    @pl.when(s == 0)
    def _():
        act0[...] = jnp.zeros(act0.shape, BF16)

    _pipeline_part(0, hid0, act0, hid1, dn_ref.at[hi, :], xn_ref,
                   act1, up_ref.at[:, hi], acc,
                   unb_ref.at[0], unb_ref.at[1], vrb_ref.at[0], vrb_ref.at[1])

    @pl.when(jnp.logical_and(s % STEPS_PER_BLOCK == 0, s > 0))
    def _():
        h2 = h1_ref[...] + acc[...].T
        ms = jnp.mean(h2 * h2, axis=-1, keepdims=True)
        out_ref[...] = h2 * lax.rsqrt(ms + RMS_EPS) * gf_ref[...]
        acc[...] = jnp.zeros(acc.shape, F32)


def _expert_call(xn2t, down, up_t, un, vr, h1, gf):
    n_blocks = SEQ // PEER_ROWS
    n_steps = n_blocks * STEPS_PER_BLOCK + 1
    last = n_blocks - 1
    halves = PEER_EXPERTS // HALF_CHUNK

    def blk(f):
        return lambda s: jnp.clip(f(s) // halves, 0, last)

    prev_half = lambda s: 2 * s - 1
    this_half = lambda s: 2 * s

    def key_spec(f):
        return pl.BlockSpec((2, PEER_HEADS, PEER_TILES, 2 * HALF_KEYS, LANES),
                            lambda s: (0, 0, blk(f)(s), (f(s) % halves) // 2, 0))

    def full_spec(f):
        return pl.BlockSpec((2, PEER_HEADS, PEER_TILES, PEER_NKEYS // 2, LANES),
                            lambda s: (0, 0, blk(f)(s), 0, 0))

    done_blk = lambda s: (jnp.maximum(s - 1, 0) // STEPS_PER_BLOCK, 0)
    return pl.pallas_call(
        _expert_kernel,
        grid=(n_steps,),
        in_specs=[
            pl.BlockSpec((D_MODEL, PEER_ROWS), lambda s: (0, blk(this_half)(s))),
            pl.BlockSpec((2 * HALF_CHUNK, D_MODEL), lambda s: (s % STEPS_PER_BLOCK, 0)),
            pl.BlockSpec((None, D_MODEL, 2 * HALF_CHUNK),
                         lambda s: ((s + STEPS_PER_BLOCK - 1) % STEPS_PER_BLOCK, 0, 0)),
            key_spec(prev_half), full_spec(prev_half),
            key_spec(this_half), full_spec(this_half),
            pl.BlockSpec((PEER_ROWS, D_MODEL), done_blk, pipeline_mode=pl.Buffered(1)),
            pl.BlockSpec((1, D_MODEL), lambda s: (0, 0)),
        ],
        out_specs=pl.BlockSpec((PEER_ROWS, D_MODEL), done_blk),
        out_shape=jax.ShapeDtypeStruct((SEQ, D_MODEL), F32),
        scratch_shapes=[
            pltpu.VMEM((HALF_CHUNK, PEER_ROWS), F32),
            pltpu.VMEM((HALF_CHUNK, PEER_ROWS), F32),
            pltpu.VMEM((HALF_CHUNK, PEER_ROWS), BF16),
            pltpu.VMEM((HALF_CHUNK, PEER_ROWS), BF16),
            pltpu.VMEM((D_MODEL, PEER_ROWS), F32),
        ],
        compiler_params=pltpu.CompilerParams(
            dimension_semantics=("arbitrary",), vmem_limit_bytes=VMEM_LIMIT),
        name="peer_experts",
    )(xn2t, down, up_t, un, vr, un, vr, h1, gf)


def kernel(x, meta_tokens, mix_norm_g, w_in, pool_w, pool_scale, conv_dw_w, conv_dw_b, conv_ln_g, conv_ln_b, conv_pw_w, conv_pw_b, w_out, ffn_norm_g, peer_wq, peer_keys, peer_down, peer_up, final_norm_g):
    assert x.shape == (1, SEQ, D_MODEL) and w_in.shape[0] == 1
    row = lambda a: a.reshape(1, -1).astype(F32)
    h1, xn2, up_b = _mixer_call(
        x[0], meta_tokens, row(mix_norm_g[0]), w_in[0].astype(BF16),
        pool_w[0].astype(BF16), row(pool_scale[0]), conv_dw_w[0], row(conv_dw_b[0]),
        row(conv_ln_g[0]), row(conv_ln_b[0]), conv_pw_w[0].astype(BF16),
        row(conv_pw_b[0]), w_out[0].astype(BF16), row(ffn_norm_g[0]), peer_up[0])
    un, vr, down_b = _route_call(xn2, peer_wq[0].T.astype(BF16),
                                      peer_keys[0].astype(BF16), peer_down[0])
    up_t = up_b.reshape(-1, 2 * HALF_CHUNK, D_MODEL).transpose(0, 2, 1)
    out = _expert_call(xn2, down_b, up_t, un, vr, h1, row(final_norm_g))
    return out[None]
```

```python
import functools
import math

import jax
import jax.numpy as jnp
from jax import lax
from jax.experimental import pallas as pl
from jax.experimental.pallas import tpu as pltpu

F32 = jnp.float32
BF16 = jnp.bfloat16

D_MODEL = 2048
SEQ = 8192
N_META = 16
POOL_WIDTH = 1024
CONV_WIDTH = 1024
IN_WIDTH = POOL_WIDTH + 2 * CONV_WIDTH
POOL_WINDOWS = (2, 4, 8, 16)
POOL_GROUP_DIM = 256
CONV_KSIZE = 31
PEER_HEADS = 8
PEER_NKEYS = 128
PEER_EXPERTS = PEER_NKEYS * PEER_NKEYS
PEER_HALF = 128
PEER_QW = PEER_HEADS * 2 * PEER_HALF
PEER_TOPK = 16
RMS_EPS = 1e-6
LN_EPS = 1e-5

LANES = 128
SUBLANES = 8
VMEM_LIMIT = 58 * 1024 * 1024

MIX_ROWS = 256
POOL_HALO = 16
CONV_HALO = 32
CONV_CHUNK = 128
IN_K_PIECE = 512
PEER_ROWS = 512
PEER_TILES = PEER_ROWS // LANES
HALF_CHUNK = 512
HALF_KEYS = HALF_CHUNK // PEER_NKEYS
STEPS_PER_BLOCK = PEER_EXPERTS // (2 * HALF_CHUNK)
MXU_PIECE = 256
HID_K_PIECE = 512
Q_K_PIECE = 512
UP_M_PIECE = 256
NEG_INF = float("-inf")
_RANK_TAG_STEP = 2.0 ** 104
_RANK_TAGS = [-((2.0 ** 24 - 16 + k) * _RANK_TAG_STEP) for k in range(PEER_TOPK)]


def _const_spec(shape):
    zeros = (0,) * len(shape)
    return pl.BlockSpec(shape, lambda *_: zeros, pipeline_mode=pl.Buffered(1))


def _mixer_kernel(x_ref, meta_ref, g1_ref, win_ref, poolw_ref, pscale_ref,
                  dww_ref, dwb_ref, lng_ref, lnb_ref, pww_ref, pwb_ref,
                  wout_ref, g2_ref, up_ref,
                  h1_ref, xn2_ref, upb_ref,
                  xp_ext, glu_ext, glu_prev, x_prev, cv_scr, ya_scr, shift_scr,
                  hn_scr, proj_scr):
    step = pl.program_id(0)
    rows = MIX_ROWS

    upb_ref[...] = up_ref[...].astype(BF16)

    def normed(tok):
        ms = jnp.mean(tok * tok, axis=-1, keepdims=True)
        return (tok * lax.rsqrt(ms + RMS_EPS) * g1_ref[...]).astype(BF16)

    def glu_of(xa, xg):
        return xa * jax.nn.sigmoid(xg)

    @pl.when(step == 0)
    def _():
        pm = jnp.dot(normed(meta_ref[...]), win_ref[...], preferred_element_type=F32)
        xp_ext[0:POOL_HALO, :] = pm[:, :POOL_WIDTH]
        glu_ext[0:CONV_HALO - N_META, :] = jnp.zeros((CONV_HALO - N_META, CONV_WIDTH), F32)
        glu_ext[CONV_HALO - N_META:CONV_HALO, :] = glu_of(
            pm[:, POOL_WIDTH:POOL_WIDTH + CONV_WIDTH], pm[:, POOL_WIDTH + CONV_WIDTH:])
        glu_prev[...] = jnp.zeros(glu_prev.shape, F32)
        x_prev[...] = jnp.zeros(x_prev.shape, F32)
        ya_scr[...] = jnp.zeros(ya_scr.shape, BF16)

    @pl.when(step > 0)
    def _():
        xp_ext[0:POOL_HALO, :] = xp_ext[rows:rows + POOL_HALO, :]
        glu_ext[0:CONV_HALO, :] = glu_ext[rows:rows + CONV_HALO, :]

    hn_scr[...] = normed(x_ref[...])

    def in_piece(j, kk):
        cols = slice(j * MXU_PIECE, (j + 1) * MXU_PIECE)
        feat = slice(kk * IN_K_PIECE, (kk + 1) * IN_K_PIECE)
        part = jnp.dot(hn_scr[:, feat], win_ref[feat, cols], preferred_element_type=F32)
        if kk == 0:
            proj_scr[:, cols] = part
        else:
            proj_scr[:, cols] += part

    mxu_pieces = [functools.partial(in_piece, j, kk)
                  for j in range(IN_WIDTH // MXU_PIECE) for kk in range(D_MODEL // IN_K_PIECE)]

    span = CONV_CHUNK + CONV_HALO - SUBLANES
    lane_tiles = CONV_WIDTH // LANES
    n_conv = (rows // CONV_CHUNK) * lane_tiles
    for r in range(rows // CONV_CHUNK):
        r0 = r * CONV_CHUNK
        window = glu_prev[r0:r0 + CONV_CHUNK + CONV_HALO, :]
        for b in range(1, SUBLANES):
            shift_scr[b - 1] = window[b:b + span, :]
        for lt in range(lane_tiles):
            piece = r * lane_tiles + lt
            first = -(-piece * len(mxu_pieces) // n_conv)
            stop = -(-(piece + 1) * len(mxu_pieces) // n_conv)
            for mxu_piece in mxu_pieces[first:stop]:
                mxu_piece()
            lanes = slice(lt * LANES, (lt + 1) * LANES)
            acc = jnp.broadcast_to(dwb_ref[:, lanes], (CONV_CHUNK, LANES))
            for k in range(CONV_KSIZE):
                a, b = divmod(CONV_HALO - (CONV_KSIZE - 1) + k, SUBLANES)
                if b == 0:
                    src = glu_prev[r0 + a * SUBLANES:r0 + a * SUBLANES + CONV_CHUNK, lanes]
                else:
                    src = shift_scr[b - 1, a * SUBLANES:a * SUBLANES + CONV_CHUNK, lanes]
                acc = acc + dww_ref[k:k + 1, lanes] * src
            cv_scr[r0:r0 + CONV_CHUNK, lanes] = acc

    @pl.when(step == 0)
    def _():
        cv_scr[0:SUBLANES, 0:LANES] = jnp.zeros((SUBLANES, LANES), F32)

    cv = cv_scr[...]
    mu = jnp.mean(cv, axis=-1, keepdims=True)
    xc = cv - mu
    var = jnp.mean(xc * xc, axis=-1, keepdims=True)
    ln = xc * lax.rsqrt(var + LN_EPS) * lng_ref[...] + lnb_ref[...]
    sw = ln * jax.nn.sigmoid(ln)
    yb = jnp.dot(sw.astype(BF16), pww_ref[...], preferred_element_type=F32) + pwb_ref[...]
    h1 = x_prev[...] + (
        jnp.dot(ya_scr[...], wout_ref[:POOL_WIDTH, :], preferred_element_type=F32)
        + jnp.dot(yb.astype(BF16), wout_ref[POOL_WIDTH:, :], preferred_element_type=F32))
    h1_ref[...] = h1
    ms = jnp.mean(h1 * h1, axis=-1, keepdims=True)
    xn2_ref[...] = (h1 * lax.rsqrt(ms + RMS_EPS) * g2_ref[...]).T.astype(BF16)

    xp_ext[POOL_HALO:POOL_HALO + rows, :] = proj_scr[:, :POOL_WIDTH]
    glu_ext[CONV_HALO:CONV_HALO + rows, :] = glu_of(
        proj_scr[:, POOL_WIDTH:POOL_WIDTH + CONV_WIDTH], proj_scr[:, POOL_WIDTH + CONV_WIDTH:])
    for g, win in enumerate(POOL_WINDOWS):
        c0 = g * POOL_GROUP_DIM
        cols = slice(c0, c0 + POOL_GROUP_DIM)
        cur = xp_ext[POOL_HALO:POOL_HALO + rows, cols]
        acc = cur
        for k in range(1, win):
            acc = acc + xp_ext[POOL_HALO - k:POOL_HALO - k + rows, cols]
        pooled = acc * (1.0 / win) - cur
        ya = jnp.dot(pooled.astype(BF16), poolw_ref[g], preferred_element_type=F32)
        ya_scr[:, cols] = (ya * pscale_ref[:, cols]).astype(BF16)

    glu_prev[...] = glu_ext[...]
    x_prev[...] = x_ref[...]


def _mixer_call(x, meta, g1, w_in, pool_w, pool_scale, dw_w, dw_b, ln_g, ln_b,
                pw_w, pw_b, w_out, g2, up):
    n_blocks = SEQ // MIX_ROWS
    last = n_blocks - 1
    head_blk = lambda i: (jnp.minimum(i, last), 0)
    tail_blk = lambda i: (jnp.maximum(i - 1, 0), 0)
    table_spec = pl.BlockSpec((PEER_EXPERTS // n_blocks, D_MODEL), head_blk)
    return pl.pallas_call(
        _mixer_kernel,
        grid=(n_blocks + 1,),
        in_specs=[
            pl.BlockSpec((MIX_ROWS, D_MODEL), head_blk),
            _const_spec((N_META, D_MODEL)),
            _const_spec((1, D_MODEL)),
            _const_spec((D_MODEL, IN_WIDTH)),
            _const_spec((len(POOL_WINDOWS), POOL_GROUP_DIM, POOL_GROUP_DIM)),
            _const_spec((1, POOL_WIDTH)),
            _const_spec((CONV_KSIZE, CONV_WIDTH)),
            _const_spec((1, CONV_WIDTH)),
            _const_spec((1, CONV_WIDTH)),
            _const_spec((1, CONV_WIDTH)),
            _const_spec((CONV_WIDTH, CONV_WIDTH)),
            _const_spec((1, CONV_WIDTH)),
            _const_spec((D_MODEL, D_MODEL)),
            _const_spec((1, D_MODEL)),
            table_spec,
        ],
        out_specs=[pl.BlockSpec((MIX_ROWS, D_MODEL), tail_blk),
                   pl.BlockSpec((D_MODEL, MIX_ROWS), lambda i: tail_blk(i)[::-1]),
                   table_spec],
        out_shape=[jax.ShapeDtypeStruct((SEQ, D_MODEL), F32),
                   jax.ShapeDtypeStruct((D_MODEL, SEQ), BF16),
                   jax.ShapeDtypeStruct((PEER_EXPERTS, D_MODEL), BF16)],
        scratch_shapes=[
            pltpu.VMEM((POOL_HALO + MIX_ROWS, POOL_WIDTH), F32),
            pltpu.VMEM((CONV_HALO + MIX_ROWS, CONV_WIDTH), F32),
            pltpu.VMEM((CONV_HALO + MIX_ROWS, CONV_WIDTH), F32),
            pltpu.VMEM((MIX_ROWS, D_MODEL), F32),
            pltpu.VMEM((MIX_ROWS, CONV_WIDTH), F32),
            pltpu.VMEM((MIX_ROWS, POOL_WIDTH), BF16),
            pltpu.VMEM((SUBLANES - 1, CONV_CHUNK + CONV_HALO - SUBLANES, CONV_WIDTH), F32),
            pltpu.VMEM((MIX_ROWS, D_MODEL), BF16),
            pltpu.VMEM((MIX_ROWS, IN_WIDTH), F32),
        ],
        compiler_params=pltpu.CompilerParams(
            dimension_semantics=("arbitrary",), vmem_limit_bytes=VMEM_LIMIT),
        name="mixer",
    )(x, meta, g1, w_in, pool_w, pool_scale, dw_w, dw_b, ln_g, ln_b, pw_w, pw_b, w_out, g2, up)


def _top16_pair(scores, sv_scr):
    cur = scores
    for k in range(PEER_TOPK):
        m = jnp.max(cur, axis=1, keepdims=True)
        cur = jnp.where(cur == m, _RANK_TAGS[k], cur)
        sv_scr[:, k:k + 1, :] = m
    rank = jnp.floor((_RANK_TAGS[0] - cur) * (1.0 / _RANK_TAG_STEP) + 0.5)
    return jnp.where(cur <= _RANK_TAGS[0], rank, float(PEER_TOPK))


def _route_tile(s_pair, sv_scr):
    ranks = _top16_pair(s_pair, sv_scr)
    r1, r2 = ranks[0], ranks[1]
    s1, s2 = s_pair[0], s_pair[1]
    sv1_scr, sv2_scr = sv_scr.at[0], sv_scr.at[1]
    sv2_lo = sv2_scr[0:SUBLANES, :]
    row = lax.broadcasted_iota(jnp.int32, (SUBLANES, LANES), 0)

    cands = [sv1_scr[0:1, :] + sv2_scr[...]]
    for a in range(1, SUBLANES):
        c = sv1_scr[a:a + 1, :] + sv2_lo
        cands.append(jnp.where(row < PEER_TOPK // (a + 1), c, NEG_INF))
    cands.append(sv1_scr[SUBLANES:PEER_TOPK, :] + sv2_scr[0:1, :])

    top = cands[0][0:1, :]
    cur = list(cands)
    thr = top
    for k in range(PEER_TOPK):
        m = cur[0][0:SUBLANES, :]
        m = jnp.maximum(m, cur[0][SUBLANES:, :])
        for c in cur[1:]:
            m = jnp.maximum(m, c)
        thr = jnp.max(m, axis=0, keepdims=True)
        cur = [jnp.where(c == thr, NEG_INF, c) for c in cur]

    zsum = jnp.zeros((1, LANES), F32)
    counts = []
    for a, c in enumerate(cands):
        sel = c >= thr
        e = jnp.where(sel, jnp.exp(c - top), 0.0)
        zsum = zsum + jnp.sum(e, axis=0, keepdims=True)
        if a < SUBLANES:
            counts.append(jnp.sum(sel.astype(F32), axis=0, keepdims=True))
        else:
            tail = sel.astype(F32)
            counts.extend(tail[b:b + 1, :] for b in range(SUBLANES))

    r1_b = r1.astype(BF16)
    n_b = jnp.zeros(s1.shape, BF16)
    for a in range(PEER_TOPK):
        count_b = jnp.broadcast_to(counts[a].astype(BF16), s1.shape)
        n_b = jnp.where(r1_b == jnp.asarray(a, BF16), count_b, n_b)
    n = n_b.astype(F32)

    inv_z = 1.0 / zsum
    u = jnp.where(r1 < float(PEER_TOPK), jnp.exp(s1 - sv1_scr[0:1, :]), 0.0)
    v = jnp.where(r2 < float(PEER_TOPK), jnp.exp(s2 - sv2_scr[0:1, :]) * inv_z, 0.0)
    return u, v, n, r2


def _pack_bf16(x):
    return pltpu.bitcast(x.astype(BF16), jnp.uint32)


def _route_kernel(xn_ref, wqt_ref, keys_ref, down_ref,
                  un_ref, vr_ref, downb_ref,
                  q_even, q_odd, qacc, s_even, s_odd, sv_scr):
    downb_ref[...] = down_ref[...].astype(BF16)

    q_rows = 2 * PEER_HALF
    n_kk = D_MODEL // Q_K_PIECE
    assert n_kk == PEER_TILES

    def q_piece(h, kk, q_ref):
        r0 = pl.multiple_of(h * q_rows, q_rows)
        feat = slice(kk * Q_K_PIECE, (kk + 1) * Q_K_PIECE)
        part = jnp.dot(wqt_ref[pl.ds(r0, q_rows), feat], xn_ref[feat, :],
                       preferred_element_type=F32)
        if kk == 0:
            qacc[...] = part
        elif kk < n_kk - 1:
            qacc[...] += part
        else:
            q_ref[...] = (qacc[...] + part).astype(BF16)

    def scores(h, p, q_ref, s_ref):
        s_ref[p] = jnp.dot(keys_ref[h, p], q_ref[p * PEER_HALF:(p + 1) * PEER_HALF, :],
                           preferred_element_type=F32)

    def head(h, s_ref, q_new_ref, q_nxt_ref, s_nxt_ref):
        last = PEER_HEADS - 1
        for t in range(PEER_TILES):
            q_piece(jnp.minimum(h + 2, last), t, q_new_ref)
            if t < 2:
                scores(jnp.minimum(h + 1, last), t, q_nxt_ref, s_nxt_ref)
            lanes = slice(t * LANES, (t + 1) * LANES)
            u, v, n, r2 = _route_tile(s_ref[:, :, lanes], sv_scr)
            un_ref[0, h, t] = u
            un_ref[1, h, t] = n
            vr_ref[0, h, t] = _pack_bf16(v)
            vr_ref[1, h, t] = _pack_bf16(r2)

    for kk in range(n_kk):
        q_piece(0, kk, q_even)
    for kk in range(n_kk):
        q_piece(1, kk, q_odd)
    scores(0, 0, q_even, s_even)
    scores(0, 1, q_even, s_even)

    def head_pair(g, carry):
        h = 2 * g
        head(h, s_even, q_even, q_odd, s_odd)
        head(h + 1, s_odd, q_odd, q_even, s_even)
        return carry

    lax.fori_loop(0, PEER_HEADS // 2, head_pair, 0)


def _route_call(xn2, wq_t, keys, down):
    n_steps = SEQ // PEER_ROWS
    def fac(rows, dtype):
        shape = jax.ShapeDtypeStruct((2, PEER_HEADS, SEQ // LANES, rows, LANES), dtype)
        spec = pl.BlockSpec((2, PEER_HEADS, PEER_TILES, rows, LANES),
                            lambda i: (0, 0, i, 0, 0))
        return shape, spec

    facs = [fac(PEER_NKEYS, F32), fac(PEER_NKEYS // 2, jnp.uint32)]
    table_spec = pl.BlockSpec((PEER_EXPERTS // n_steps, D_MODEL), lambda i: (i, 0))
    return pl.pallas_call(
        _route_kernel,
        grid=(n_steps,),
        in_specs=[
            pl.BlockSpec((D_MODEL, PEER_ROWS), lambda i: (0, i)),
            _const_spec((PEER_QW, D_MODEL)),
            _const_spec((PEER_HEADS, 2, PEER_NKEYS, PEER_HALF)),
            table_spec,
        ],
        out_specs=[spec for _, spec in facs] + [table_spec],
        out_shape=[shape for shape, _ in facs]
        + [jax.ShapeDtypeStruct((PEER_EXPERTS, D_MODEL), BF16)],
        scratch_shapes=[
            pltpu.VMEM((2 * PEER_HALF, PEER_ROWS), BF16),
            pltpu.VMEM((2 * PEER_HALF, PEER_ROWS), BF16),
            pltpu.VMEM((2 * PEER_HALF, PEER_ROWS), F32),
            pltpu.VMEM((2, PEER_NKEYS, PEER_ROWS), F32),
            pltpu.VMEM((2, PEER_NKEYS, PEER_ROWS), F32),
            pltpu.VMEM((2, PEER_TOPK, LANES), F32),
        ],
        compiler_params=pltpu.CompilerParams(
            dimension_semantics=("arbitrary",), vmem_limit_bytes=VMEM_LIMIT),
        name="peer_route",
    )(xn2, wq_t, keys, down)


def _hidden_piece(p, kk, hid_ref, down_ref, xn_ref):
    toks = slice(p * MXU_PIECE, (p + 1) * MXU_PIECE)
    feat = slice(kk * HID_K_PIECE, (kk + 1) * HID_K_PIECE)
    part = jnp.dot(down_ref[:, feat], xn_ref[feat, toks], preferred_element_type=F32)
    if kk == 0:
        hid_ref[:, toks] = part
    else:
        hid_ref[:, toks] += part


def _up_piece(p, m, acc_ref, upt_ref, act_ref):
    toks = slice(p * MXU_PIECE, (p + 1) * MXU_PIECE)
    rows = slice(m * UP_M_PIECE, (m + 1) * UP_M_PIECE)
    acc_ref[rows, toks] += jnp.dot(upt_ref[rows, :], act_ref[:, toks],
                                   preferred_element_type=F32)


def _gate_tile(krow, il, t, hid_ref, act_ref, u_ref, n_ref, v_ref, r2_ref):
    erows = slice(il * PEER_NKEYS, (il + 1) * PEER_NKEYS)
    lanes = slice(t * LANES, (t + 1) * LANES)
    tile = (PEER_NKEYS, LANES)
    w = jnp.zeros(tile, BF16)
    for h in range(PEER_HEADS):
        ui = jnp.broadcast_to(u_ref[h, t, krow:krow + 1, :].astype(BF16), tile)
        ni = jnp.broadcast_to(n_ref[h, t, krow:krow + 1, :].astype(BF16), tile)
        vt = pltpu.bitcast(v_ref[h, t], BF16)
        rt = pltpu.bitcast(r2_ref[h, t], BF16)
        w = w + jnp.where(rt < ni, vt * ui, jnp.zeros((), BF16))
    hid = hid_ref[erows, lanes]
    gelu = 0.5 * hid * (1.0 + lax.erf(hid * (1.0 / math.sqrt(2.0))))
    act_ref[erows, lanes] = w * gelu.astype(BF16)


def _pipeline_part(krow0, hid_cur, act_cur, hid_next, down_ref, xn_ref,
                   act_prev, upt_ref, acc_ref, u_ref, n_ref, v_ref, r2_ref):
    tiles = [(il, t) for il in range(HALF_KEYS) for t in range(PEER_TILES)]
    n_tok = PEER_ROWS // MXU_PIECE
    ups = [functools.partial(_up_piece, p, m, acc_ref, upt_ref, act_prev)
           for p in range(n_tok) for m in range(D_MODEL // UP_M_PIECE)]
    hids = [functools.partial(_hidden_piece, p, kk, hid_next, down_ref, xn_ref)
            for p in range(n_tok) for kk in range(D_MODEL // HID_K_PIECE)]
    for k, (il, t) in enumerate(tiles):
        for pieces in (ups, hids):
            lo = -(-k * len(pieces) // len(tiles))
            hi = -(-(k + 1) * len(pieces) // len(tiles))
            for piece in pieces[lo:hi]:
                piece()
        _gate_tile(krow0 + il, il, t, hid_cur, act_cur, u_ref, n_ref, v_ref, r2_ref)


def _expert_kernel(xn_ref, dn_ref, up_ref, una_ref, vra_ref, unb_ref, vrb_ref,
                   h1_ref, gf_ref, out_ref, hid0, hid1, act0, act1, acc):
    s = pl.program_id(0)
    lo, hi = pl.ds(0, HALF_CHUNK), pl.ds(HALF_CHUNK, HALF_CHUNK)

    @pl.when(s == 0)
    def _():
        hid1[...] = jnp.zeros(hid1.shape, F32)
        act0[...] = jnp.zeros(act0.shape, BF16)
        act1[...] = jnp.zeros(act1.shape, BF16)
        acc[...] = jnp.zeros(acc.shape, F32)

    _pipeline_part(HALF_KEYS, hid1, act1, hid0, dn_ref.at[lo, :], xn_ref,
                   act0, up_ref.at[:, lo], acc,
                   una_ref.at[0], una_ref.at[1], vra_ref.at[0], vra_ref.at[1])

    @pl.when(s == 0)
    def _():
        act0[...] = jnp.zeros(act0.shape, BF16)

    _pipeline_part(0, hid0, act0, hid1, dn_ref.at[hi, :], xn_ref,
                   act1, up_ref.at[:, hi], acc,
                   unb_ref.at[0], unb_ref.at[1], vrb_ref.at[0], vrb_ref.at[1])

    @pl.when(jnp.logical_and(s % STEPS_PER_BLOCK == 0, s > 0))
    def _():
        h2 = h1_ref[...] + acc[...].T
        ms = jnp.mean(h2 * h2, axis=-1, keepdims=True)
        out_ref[...] = h2 * lax.rsqrt(ms + RMS_EPS) * gf_ref[...]
        acc[...] = jnp.zeros(acc.shape, F32)


def _expert_call(xn2t, down, up_t, un, vr, h1, gf):
    n_blocks = SEQ // PEER_ROWS
    n_steps = n_blocks * STEPS_PER_BLOCK + 1
    last = n_blocks - 1
    halves = PEER_EXPERTS // HALF_CHUNK

    def blk(f):
        return lambda s: jnp.clip(f(s) // halves, 0, last)

    prev_half = lambda s: 2 * s - 1
    this_half = lambda s: 2 * s

    def key_spec(f):
        return pl.BlockSpec((2, PEER_HEADS, PEER_TILES, 2 * HALF_KEYS, LANES),
                            lambda s: (0, 0, blk(f)(s), (f(s) % halves) // 2, 0))

    def full_spec(f):
        return pl.BlockSpec((2, PEER_HEADS, PEER_TILES, PEER_NKEYS // 2, LANES),
                            lambda s: (0, 0, blk(f)(s), 0, 0))

    done_blk = lambda s: (jnp.maximum(s - 1, 0) // STEPS_PER_BLOCK, 0)
    return pl.pallas_call(
        _expert_kernel,
        grid=(n_steps,),
        in_specs=[
            pl.BlockSpec((D_MODEL, PEER_ROWS), lambda s: (0, blk(this_half)(s))),
            pl.BlockSpec((2 * HALF_CHUNK, D_MODEL), lambda s: (s % STEPS_PER_BLOCK, 0)),
            pl.BlockSpec((None, D_MODEL, 2 * HALF_CHUNK),
                         lambda s: ((s + STEPS_PER_BLOCK - 1) % STEPS_PER_BLOCK, 0, 0)),
            key_spec(prev_half), full_spec(prev_half),
            key_spec(this_half), full_spec(this_half),
            pl.BlockSpec((PEER_ROWS, D_MODEL), done_blk, pipeline_mode=pl.Buffered(1)),
            pl.BlockSpec((1, D_MODEL), lambda s: (0, 0)),
        ],
        out_specs=pl.BlockSpec((PEER_ROWS, D_MODEL), done_blk),
        out_shape=jax.ShapeDtypeStruct((SEQ, D_MODEL), F32),
        scratch_shapes=[
            pltpu.VMEM((HALF_CHUNK, PEER_ROWS), F32),
            pltpu.VMEM((HALF_CHUNK, PEER_ROWS), F32),
            pltpu.VMEM((HALF_CHUNK, PEER_ROWS), BF16),
            pltpu.VMEM((HALF_CHUNK, PEER_ROWS), BF16),
            pltpu.VMEM((D_MODEL, PEER_ROWS), F32),
        ],
        compiler_params=pltpu.CompilerParams(
            dimension_semantics=("arbitrary",), vmem_limit_bytes=VMEM_LIMIT),
        name="peer_experts",
    )(xn2t, down, up_t, un, vr, un, vr, h1, gf)


def kernel(x, meta_tokens, mix_norm_g, w_in, pool_w, pool_scale, conv_dw_w, conv_dw_b, conv_ln_g, conv_ln_b, conv_pw_w, conv_pw_b, w_out, ffn_norm_g, peer_wq, peer_keys, peer_down, peer_up, final_norm_g):
    assert x.shape == (1, SEQ, D_MODEL) and w_in.shape[0] == 1
    row = lambda a: a.reshape(1, -1).astype(F32)
    h1, xn2, up_b = _mixer_call(
        x[0], meta_tokens, row(mix_norm_g[0]), w_in[0].astype(BF16),
        pool_w[0].astype(BF16), row(pool_scale[0]), conv_dw_w[0], row(conv_dw_b[0]),
        row(conv_ln_g[0]), row(conv_ln_b[0]), conv_pw_w[0].astype(BF16),
        row(conv_pw_b[0]), w_out[0].astype(BF16), row(ffn_norm_g[0]), peer_up[0])
    un, vr, down_b = _route_call(xn2, peer_wq[0].T.astype(BF16),
                                      peer_keys[0].astype(BF16), peer_down[0])
    up_t = up_b.reshape(-1, 2 * HALF_CHUNK, D_MODEL).transpose(0, 2, 1)
    out = _expert_call(xn2, down_b, up_t, un, vr, h1, row(final_norm_g))
    return out[None]
```
